```python
import jax
import jax.numpy as jnp
from jax import lax
import numpy as np

D_MODEL = 1024
BATCH = 32
SEQ = 2048
DEPTH = 1

GRID_W = 64
CTX_LEN = 256
D_CONV = 512
N_NA_HEADS = 8
HEAD_DIM = 64
D_NA = N_NA_HEADS * HEAD_DIM
D_MIX = D_CONV + D_NA
D_IN = 2 * D_CONV + 3 * D_NA
CONV_WIDTH = 31
NA_ROWS = 8
NA_COLS = 16
NA_Q_COLS = 16
NA_KEY_COLS = 2 * NA_COLS
PEER_HEADS = 8
PEER_KEYS = 128
N_EXPERTS = PEER_KEYS * PEER_KEYS
PEER_DK = 256
PEER_TOPK = 16
PEER_CHUNK = 128
N_MOD = 6
EPS = 1e-6
NEG_INF = -1e30

kernel_name = "hybrid_conv_natten_peer_dit_block"


def rms_norm(x, g):
    xf = x.astype(jnp.float32)
    y = xf * lax.rsqrt(jnp.mean(xf * xf, axis=-1, keepdims=True) + EPS)
    return y.astype(x.dtype) * g


def layer_norm(x, g, b):
    xf = x.astype(jnp.float32)
    mu = jnp.mean(xf, axis=-1, keepdims=True)
    var = jnp.mean(jnp.square(xf - mu), axis=-1, keepdims=True)
    return ((xf - mu) * lax.rsqrt(var + EPS)).astype(x.dtype) * g + b


def conformer_conv(u, conv_w, conv_b, cn_g, cn_b):
    a, gate = jnp.split(u, 2, axis=-1)
    h = a * jax.nn.sigmoid(gate)
    h = lax.conv_general_dilated(
        h, conv_w[:, None, :].astype(h.dtype), window_strides=(1,),
        padding=[(CONV_WIDTH // 2, CONV_WIDTH // 2)],
        dimension_numbers=("NWC", "WIO", "NWC"), feature_group_count=D_CONV) + conv_b
    return jax.nn.silu(layer_norm(h, cn_g, cn_b))


def _na_tables(rows):
    wr = min(NA_ROWS, rows)
    ncb = GRID_W // NA_Q_COLS
    r = np.arange(rows)
    row_start = np.clip(r - wr // 2, 0, rows - wr)
    key_rows = row_start[:, None] + np.arange(wr)[None, :]
    j = np.arange(ncb)
    col_start = np.clip(j * NA_Q_COLS - NA_COLS // 2, 0, GRID_W - NA_KEY_COLS)
    key_cols = col_start[:, None] + np.arange(NA_KEY_COLS)[None, :]
    key_idx = key_rows[:, None, :, None] * GRID_W + key_cols[None, :, None, :]
    q_cols = j[:, None] * NA_Q_COLS + np.arange(NA_Q_COLS)[None, :]
    win_start = np.clip(q_cols - NA_COLS // 2, 0, GRID_W - NA_COLS)
    kc = key_cols[:, None, :]
    col_ok = (kc >= win_start[..., None]) & (kc < win_start[..., None] + NA_COLS)
    dr = key_rows - r[:, None] + NA_ROWS - 1
    dc = np.clip(kc - q_cols[..., None] + NA_COLS - 1, 0, 2 * NA_COLS - 2)
    return key_idx.reshape(rows, ncb, wr * NA_KEY_COLS).astype(np.int32), col_ok, dr, dc, wr


def neighbourhood_attention(q, k, v, k_ctx, v_ctx, rpb):
    b, l = q.shape[0], q.shape[1]
    rows = l // GRID_W
    ncb = GRID_W // NA_Q_COLS
    key_idx, col_ok, dr, dc, wr = _na_tables(rows)
    n_keys = wr * NA_KEY_COLS
    bias = rpb.astype(jnp.float32)[:, dr[:, None, None, :, None], dc[None, :, :, None, :]]
    bias = jnp.where(col_ok[None, None, :, :, None, :], bias, NEG_INF)
    bias = bias.reshape(N_NA_HEADS, rows, ncb, NA_Q_COLS, n_keys).transpose(1, 0, 2, 3, 4)
    q_rows = q.reshape(b, rows, ncb, NA_Q_COLS, N_NA_HEADS, HEAD_DIM).transpose(1, 0, 2, 3, 4, 5)
    scale = HEAD_DIM ** -0.5

    def row_fn(args):
        q_r, idx_r, bias_r = args
        kb = k[:, idx_r]
        vb = v[:, idx_r]
        s_loc = jnp.einsum("bjqhd,bjkhd->bhjqk", q_r, kb).astype(jnp.float32) * scale + bias_r
        s_ctx = jnp.einsum("bjqhd,bchd->bhjqc", q_r, k_ctx).astype(jnp.float32) * scale
        p = jax.nn.softmax(jnp.concatenate([s_loc, s_ctx], axis=-1), axis=-1).astype(v.dtype)
        return (jnp.einsum("bhjqk,bjkhd->bjqhd", p[..., :n_keys], vb)
                + jnp.einsum("bhjqc,bchd->bjqhd", p[..., n_keys:], v_ctx))

    o = lax.map(row_fn, (q_rows, jnp.asarray(key_idx), bias))
    return o.transpose(1, 0, 2, 3, 4, 5).reshape(b, l, D_NA)


def context_attention(q, k, v):
    b, lc = q.shape[0], q.shape[1]
    s = jnp.einsum("bqhd,bkhd->bhqk", q, k).astype(jnp.float32) * HEAD_DIM ** -0.5
    p = jax.nn.softmax(s, axis=-1).astype(v.dtype)
    return jnp.einsum("bhqk,bkhd->bqhd", p, v).reshape(b, lc, D_NA)


def split_heads(t):
    return t.reshape(t.shape[0], t.shape[1], N_NA_HEADS, HEAD_DIM)


def token_mixer(h_lat, h_ctx, w_in, conv_w, conv_b, cn_g, cn_b, rpb, w_out, with_ctx):
    p_lat = h_lat @ w_in
    q, k, v = [split_heads(t) for t in jnp.split(p_lat[..., 2 * D_CONV:], 3, axis=-1)]
    q_c, k_c, v_c = [split_heads(t) for t in jnp.split(h_ctx @ w_in[:, 2 * D_CONV:], 3, axis=-1)]
    conv_lat = conformer_conv(p_lat[..., :2 * D_CONV], conv_w, conv_b, cn_g, cn_b)
    na_lat = neighbourhood_attention(q, k, v, k_c, v_c, rpb)
    y_lat = jnp.concatenate([conv_lat, na_lat], axis=-1) @ w_out
    if not with_ctx:
        return y_lat, None
    conv_ctx = conformer_conv(h_ctx @ w_in[:, :2 * D_CONV], conv_w, conv_b, cn_g, cn_b)
    y_ctx = jnp.concatenate([conv_ctx, context_attention(q_c, k_c, v_c)], axis=-1) @ w_out
    return y_lat, y_ctx


def peer_ffn(h, w_q, sub_keys, u_emb, v_emb):
    shape = h.shape
    chunks = h.reshape(-1, PEER_CHUNK, D_MODEL)

    def chunk_fn(hc):
        n = hc.shape[0]
        qh = (hc @ w_q).reshape(n, PEER_HEADS, 2, PEER_DK // 2)
        s = jnp.einsum("nhpd,hpkd->nhpk", qh, sub_keys).astype(jnp.float32)
        s1, i1 = lax.top_k(s[:, :, 0], PEER_TOPK)
        s2, i2 = lax.top_k(s[:, :, 1], PEER_TOPK)
        cand_s = (s1[..., :, None] + s2[..., None, :]).reshape(n, PEER_HEADS, PEER_TOPK * PEER_TOPK)
        cand_i = (i1[..., :, None] * PEER_KEYS + i2[..., None, :]).reshape(n, PEER_HEADS, PEER_TOPK * PEER_TOPK)
        top_s, pos = lax.top_k(cand_s, PEER_TOPK)
        idx = jnp.take_along_axis(cand_i, pos, axis=-1)
        g = jax.nn.softmax(top_s, axis=-1).astype(hc.dtype)
        act = jax.nn.gelu(jnp.einsum("nd,nhkd->nhk", hc, u_emb[idx]))
        return jnp.einsum("nhk,nhkd->nd", g * act, v_emb[idx])

    return lax.map(chunk_fn, chunks).reshape(shape)


def setup_inputs(seed: int = 0) -> dict:
    key = jax.random.key(seed)
    ks = jax.random.split(key, 21)

    def nrm(k, shape, scale):
        return jax.random.normal(k, shape, jnp.float32) * scale

    L = DEPTH
    return {
        "x": nrm(ks[0], (BATCH, SEQ, D_MODEL), 1.0),
        "c": nrm(ks[1], (BATCH, D_MODEL), 1.0),
        "ctx": nrm(ks[2], (BATCH, CTX_LEN, D_MODEL), 1.0),
        "c_ctx": nrm(ks[3], (D_MODEL,), 1.0),
        "w_ada": nrm(ks[4], (L, D_MODEL, N_MOD * D_MODEL), 0.5 * D_MODEL ** -0.5),
        "b_ada": nrm(ks[5], (L, N_MOD * D_MODEL), 0.01),
        "g_pre_mix": 1.0 + nrm(ks[6], (L, D_MODEL), 0.01),
        "g_post_mix": 1.0 + nrm(ks[7], (L, D_MODEL), 0.01),
        "g_pre_ffn": 1.0 + nrm(ks[8], (L, D_MODEL), 0.01),
        "g_post_ffn": 1.0 + nrm(ks[9], (L, D_MODEL), 0.01),
        "w_in": nrm(ks[10], (L, D_MODEL, D_IN), D_MODEL ** -0.5),
        "conv_w": nrm(ks[11], (L, CONV_WIDTH, D_CONV), CONV_WIDTH ** -0.5),
        "conv_b": nrm(ks[12], (L, D_CONV), 0.01),
        "conv_norm_g": 1.0 + nrm(ks[13], (L, D_CONV), 0.01),
        "conv_norm_b": nrm(ks[14], (L, D_CONV), 0.01),
        "na_rpb": nrm(ks[15], (L, N_NA_HEADS, 2 * NA_ROWS - 1, 2 * NA_COLS - 1), 0.1),
        "w_out": nrm(ks[16], (L, D_MIX, D_MODEL), D_MIX ** -0.5),
        "peer_wq": nrm(ks[17], (L, D_MODEL, PEER_HEADS * PEER_DK), D_MODEL ** -0.5),
        "peer_keys": nrm(ks[18], (L, PEER_HEADS, 2, PEER_KEYS, PEER_DK // 2), (PEER_DK // 2) ** -0.5),
        "peer_u": nrm(ks[19], (L, N_EXPERTS, D_MODEL), D_MODEL ** -0.5),
        "peer_v": nrm(ks[20], (L, N_EXPERTS, D_MODEL), D_MODEL ** -0.5),
    }


def reference(x, c, ctx, c_ctx, w_ada, b_ada, g_pre_mix, g_post_mix, g_pre_ffn, g_post_ffn,
              w_in, conv_w, conv_b, conv_norm_g, conv_norm_b, na_rpb, w_out,
              peer_wq, peer_keys, peer_u, peer_v):
    x_lat, x_ctx = x, ctx
    for layer in range(DEPTH):
        with_ctx = layer < DEPTH - 1
        mod_lat = jax.nn.silu(c) @ w_ada[layer] + b_ada[layer]
        mod_ctx = jax.nn.silu(c_ctx) @ w_ada[layer] + b_ada[layer]
        sh1, sc1, gt1, sh2, sc2, gt2 = jnp.split(mod_lat[:, None, :], N_MOD, axis=-1)
        csh1, csc1, cgt1, csh2, csc2, cgt2 = jnp.split(mod_ctx, N_MOD, axis=-1)

        h_lat = rms_norm(x_lat, g_pre_mix[layer]) * (1.0 + sc1) + sh1
        h_ctx = rms_norm(x_ctx, g_pre_mix[layer]) * (1.0 + csc1) + csh1
        y_lat, y_ctx = token_mixer(h_lat, h_ctx, w_in[layer], conv_w[layer], conv_b[layer],
                                   conv_norm_g[layer], conv_norm_b[layer], na_rpb[layer],
                                   w_out[layer], with_ctx)
        x_lat = x_lat + gt1 * rms_norm(y_lat, g_post_mix[layer])

        h_lat = rms_norm(x_lat, g_pre_ffn[layer]) * (1.0 + sc2) + sh2
        f_lat = peer_ffn(h_lat, peer_wq[layer], peer_keys[layer], peer_u[layer], peer_v[layer])
        x_lat = x_lat + gt2 * rms_norm(f_lat, g_post_ffn[layer])

        if with_ctx:
            x_ctx = x_ctx + cgt1 * rms_norm(y_ctx, g_post_mix[layer])
            h_ctx = rms_norm(x_ctx, g_pre_ffn[layer]) * (1.0 + csc2) + csh2
            f_ctx = peer_ffn(h_ctx, peer_wq[layer], peer_keys[layer], peer_u[layer], peer_v[layer])
            x_ctx = x_ctx + cgt2 * rms_norm(f_ctx, g_post_ffn[layer])
    return x_lat
```

```python
import functools
import math

import numpy as np
import jax
import jax.numpy as jnp
from jax import lax
from jax.experimental import pallas as pl
from jax.experimental.pallas import tpu as pltpu

F32 = jnp.float32
BF16 = jnp.bfloat16
I32 = jnp.int32

D_MODEL = 1024
GRID_W = 64
D_CONV = 512
N_NA_HEADS = 8
HEAD_DIM = 64
D_NA = N_NA_HEADS * HEAD_DIM
CONV_WIDTH = 31
CONV_HALO = 16
NA_ROWS = 8
NA_COLS = 16
PEER_HEADS = 8
PEER_KEYS = 128
PEER_HALF = 128
PEER_TOPK = 16
N_PAIRS = PEER_HEADS * PEER_TOPK
N_MOD = 6
EPS = 1e-6
NEG_INF = -1e30

LANES = 128
SUBLANES = 8
ROW_WORDS = D_MODEL // 2
ROW_SLAB = ROW_WORDS // LANES
VMEM_TABLE_LIMIT = 52 * 1024 * 1024
VMEM_LIMIT = 48 * 1024 * 1024


def _silu(x):
    return x * (1.0 / (1.0 + jnp.exp(-x)))


def _rms(x):
    return x * lax.rsqrt(jnp.mean(x * x, axis=-1, keepdims=True) + EPS)


def _params(n_grid, limit=VMEM_LIMIT):
    return pltpu.CompilerParams(dimension_semantics=("arbitrary",) * n_grid, vmem_limit_bytes=limit)


def _adaln_kernel(c_ref, w_ref, b_ref, o_ref):
    s = _silu(c_ref[...]).astype(BF16)
    o_ref[...] = jnp.dot(s, w_ref[...].astype(BF16), preferred_element_type=F32) + b_ref[...]


def _adaln(c_all, w_ada, b_ada):
    rows, d = c_all.shape
    n = w_ada.shape[1]
    tn = 1536
    return pl.pallas_call(
        _adaln_kernel,
        grid=(n // tn,),
        in_specs=[pl.BlockSpec((rows, d), lambda j: (0, 0)),
                  pl.BlockSpec((d, tn), lambda j: (0, j)),
                  pl.BlockSpec((1, tn), lambda j: (0, j))],
        out_specs=pl.BlockSpec((rows, tn), lambda j: (0, j)),
        out_shape=jax.ShapeDtypeStruct((rows, n), F32),
        compiler_params=_params(1),
        name="adaln",
    )(c_all, w_ada, b_ada.reshape(1, n))


def _norm_mod(x, g, sc, sh):
    return (_rms(x) * g) * (1.0 + sc) + sh


def _inproj_kernel(x_ref, g_ref, sc_ref, sh_ref, w_ref, hh_ref, q_ref, k_ref, v_ref):
    h = _norm_mod(x_ref[0], g_ref[...], sc_ref[0], sh_ref[0]).astype(BF16)

    def proj(j):
        return jnp.dot(h, w_ref[:, j * D_CONV:(j + 1) * D_CONV], preferred_element_type=F32)

    a = proj(0)
    gate = proj(1)
    hh_ref[0] = a * (1.0 / (1.0 + jnp.exp(-gate)))
    q_ref[0] = proj(2).astype(BF16)
    k_ref[0] = proj(3).astype(BF16)
    v_ref[0] = proj(4).astype(BF16)


def _in_proj(x, g, sc, sh, w_in_bf):
    b, l, d = x.shape
    tm = min(512, l)
    mod_spec = pl.BlockSpec((1, 1, d), lambda i, j: (i, 0, 0))
    out_spec = pl.BlockSpec((1, tm, D_CONV), lambda i, j: (i, j, 0))
    return pl.pallas_call(
        _inproj_kernel,
        grid=(b, l // tm),
        in_specs=[pl.BlockSpec((1, tm, d), lambda i, j: (i, j, 0)),
                  pl.BlockSpec((1, d), lambda i, j: (0, 0)),
                  mod_spec, mod_spec,
                  pl.BlockSpec(w_in_bf.shape, lambda i, j: (0, 0))],
        out_specs=[out_spec] * 4,
        out_shape=[jax.ShapeDtypeStruct((b, l, D_CONV), F32)] + [jax.ShapeDtypeStruct((b, l, D_NA), BF16)] * 3,
        compiler_params=_params(2),
        name="in_proj",
    )(x, g, sc, sh, w_in_bf)


def _ctxproj_kernel(x_ref, g_ref, sc_ref, sh_ref, w_ref, k_ref, v_ref):
    h = _norm_mod(x_ref[0], g_ref[...], sc_ref[...], sh_ref[...]).astype(BF16)
    k_ref[0] = jnp.dot(h, w_ref[:, :D_NA], preferred_element_type=F32).astype(BF16)
    v_ref[0] = jnp.dot(h, w_ref[:, D_NA:], preferred_element_type=F32).astype(BF16)


def _ctx_proj(ctx, g, sc, sh, w_kv_bf):
    b, lc, d = ctx.shape
    vec = pl.BlockSpec((1, d), lambda i: (0, 0))
    out_spec = pl.BlockSpec((1, lc, D_NA), lambda i: (i, 0, 0))
    return pl.pallas_call(
        _ctxproj_kernel,
        grid=(b,),
        in_specs=[pl.BlockSpec((1, lc, d), lambda i: (i, 0, 0)), vec, vec, vec,
                  pl.BlockSpec(w_kv_bf.shape, lambda i: (0, 0))],
        out_specs=[out_spec] * 2,
        out_shape=[jax.ShapeDtypeStruct((b, lc, D_NA), BF16)] * 2,
        compiler_params=_params(1),
        name="ctx_proj",
    )(ctx, g, sc, sh, w_kv_bf)


CONV_TILE = 64


def _conv_kernel(h_ref, w_ref, b_ref, g_ref, beta_ref, o_ref, pad_ref):
    l = h_ref.shape[1]
    zeros = jnp.zeros((CONV_HALO, D_CONV), F32)
    pad_ref[pl.ds(0, CONV_HALO), :] = zeros
    pad_ref[pl.ds(CONV_HALO + l, CONV_HALO), :] = zeros
    pad_ref[pl.ds(CONV_HALO, l), :] = h_ref[0]
    first = CONV_HALO - CONV_WIDTH // 2

    def tile(i, carry):
        t0 = pl.multiple_of(i * CONV_TILE, CONV_TILE)
        win = pad_ref[pl.ds(t0, CONV_TILE + 2 * CONV_HALO), :]
        acc = jnp.zeros((CONV_TILE, D_CONV), F32)
        for k in range(CONV_WIDTH):
            acc = acc + win[first + k:first + k + CONV_TILE, :] * w_ref[k:k + 1, :]
        y = acc + b_ref[...]
        mu = jnp.mean(y, axis=-1, keepdims=True)
        yc = y - mu
        var = jnp.mean(yc * yc, axis=-1, keepdims=True)
        z = (yc * lax.rsqrt(var + EPS)) * g_ref[...] + beta_ref[...]
        o_ref[0, pl.ds(t0, CONV_TILE), :] = _silu(z).astype(BF16)
        return carry

    lax.fori_loop(0, l // CONV_TILE, tile, 0)


def _conv(hh, conv_w, conv_b, cn_g, cn_b):
    b, l, c = hh.shape
    vec = pl.BlockSpec((1, c), lambda i: (0, 0))
    wpad = jnp.zeros((32, c), F32).at[:CONV_WIDTH].set(conv_w)
    return pl.pallas_call(
        _conv_kernel,
        grid=(b,),
        in_specs=[pl.BlockSpec((1, l, c), lambda i: (i, 0, 0)),
                  pl.BlockSpec((32, c), lambda i: (0, 0)), vec, vec, vec],
        out_specs=pl.BlockSpec((1, l, c), lambda i: (i, 0, 0)),
        out_shape=jax.ShapeDtypeStruct((b, l, c), BF16),
        scratch_shapes=[pltpu.VMEM((l + 2 * CONV_HALO, c), F32)],
        compiler_params=_params(1),
        name="conv",
    )(hh, wpad, conv_b.reshape(1, c), cn_g.reshape(1, c), cn_b.reshape(1, c))


def _na_bias_table(rpb, wr):
    c = np.arange(GRID_W)
    win_start = np.clip(c - NA_COLS // 2, 0, GRID_W - NA_COLS)
    kc = np.arange(GRID_W)
    ok = (kc[None, :] >= win_start[:, None]) & (kc[None, :] < win_start[:, None] + NA_COLS)
    dc = np.clip(kc[None, :] - c[:, None] + NA_COLS - 1, 0, 2 * NA_COLS - 2)
    dr = np.arange(NA_ROWS)[:, None] + np.arange(wr)[None, :]
    bias = rpb.astype(F32)[:, dr[:, None, :, None], dc[None, :, None, :]]
    bias = jnp.where(ok[None, None, :, None, :], bias, NEG_INF)
    return bias.transpose(1, 0, 2, 3, 4).reshape(NA_ROWS, N_NA_HEADS, GRID_W, wr * GRID_W)


def _row_start(r, rows, wr):
    return jnp.clip(r - wr // 2, 0, rows - wr)


def _na_kernel(q_ref, k_ref, v_ref, kc_ref, vc_ref, bias_ref, o_ref, *, rows, wr):
    r = pl.program_id(1)
    start = pl.multiple_of(_row_start(r, rows, wr) * GRID_W, GRID_W)
    nk = wr * GRID_W
    scale = HEAD_DIM ** -0.5
    q = q_ref[0]
    lane = lax.broadcasted_iota(I32, (GRID_W, LANES), 1)
    nt = (((1,), (1,)), ((), ()))
    for p in range(N_NA_HEADS // 2):
        cols = slice(p * LANES, (p + 1) * LANES)
        qp = q[:, cols]
        kp = k_ref[0, pl.ds(start, nk), cols]
        vp = v_ref[0, pl.ds(start, nk), cols]
        kcp = kc_ref[0, :, cols]
        vcp = vc_ref[0, :, cols]
        outs = []
        for hh in range(2):
            in_head = (lane >= hh * HEAD_DIM) & (lane < (hh + 1) * HEAD_DIM)
            qm = jnp.where(in_head, qp, jnp.zeros_like(qp))
            s_loc = lax.dot_general(qm, kp, nt, preferred_element_type=F32) * scale + bias_ref[0, 2 * p + hh]
            s_ctx = lax.dot_general(qm, kcp, nt, preferred_element_type=F32) * scale
            m = jnp.maximum(jnp.max(s_loc, axis=-1, keepdims=True), jnp.max(s_ctx, axis=-1, keepdims=True))
            e_loc = jnp.exp(s_loc - m)
            e_ctx = jnp.exp(s_ctx - m)
            denom = jnp.sum(e_loc, axis=-1, keepdims=True) + jnp.sum(e_ctx, axis=-1, keepdims=True)
            o = (jnp.dot(e_loc.astype(BF16), vp, preferred_element_type=F32)
                 + jnp.dot(e_ctx.astype(BF16), vcp, preferred_element_type=F32))
            outs.append(o / denom)
        o_ref[0, :, cols] = jnp.where(lane < HEAD_DIM, outs[0], outs[1]).astype(BF16)


def _na_attention(q, k, v, kc, vc, bias):
    b, l, d = q.shape
    rows = l // GRID_W
    wr = min(NA_ROWS, rows)
    assert rows >= NA_ROWS, "neighbourhood window needs at least NA_ROWS grid rows"
    lc = kc.shape[1]

    def bias_map(i, r):
        return (_row_start(r, rows, wr) - r + NA_ROWS - 1, 0, 0, 0)

    full = pl.BlockSpec((1, l, d), lambda i, r: (i, 0, 0))
    ctx = pl.BlockSpec((1, lc, d), lambda i, r: (i, 0, 0))
    return pl.pallas_call(
        functools.partial(_na_kernel, rows=rows, wr=wr),
        grid=(b, rows),
        in_specs=[pl.BlockSpec((1, GRID_W, d), lambda i, r: (i, r, 0)), full, full, ctx, ctx,
                  pl.BlockSpec((1, N_NA_HEADS, GRID_W, wr * GRID_W), bias_map)],
        out_specs=pl.BlockSpec((1, GRID_W, d), lambda i, r: (i, r, 0)),
        out_shape=jax.ShapeDtypeStruct((b, l, d), BF16),
        compiler_params=_params(2),
        name="na_attention",
    )(q, k, v, kc, vc, bias)


def _topk_rows(x, k):
    n_rows = x.shape[0]
    row = lax.broadcasted_iota(I32, x.shape, 0)
    vals, idxs = [], []
    for _ in range(k):
        m = jnp.max(x, axis=0, keepdims=True)
        i = jnp.min(jnp.where(x == m, row, n_rows), axis=0, keepdims=True)
        vals.append(m)
        idxs.append(i)
        x = jnp.where(row == i, -jnp.inf, x)
    return jnp.concatenate(vals, axis=0), jnp.concatenate(idxs, axis=0)


def _outproj_kernel(conv_ref, na_ref, x_ref, wo_ref, gpost_ref, gt_ref, gpre_ref, sc_ref, sh_ref,
                    wq_ref, keys_ref, x1_ref, idx_ref, gw_ref, q2_ref, idxt_ref):
    y = (jnp.dot(conv_ref[0], wo_ref[:D_CONV, :], preferred_element_type=F32)
         + jnp.dot(na_ref[0], wo_ref[D_CONV:, :], preferred_element_type=F32))
    x1 = x_ref[0] + gt_ref[0] * (_rms(y) * gpost_ref[...])
    x1_ref[0] = x1
    h2 = _norm_mod(x1, gpre_ref[...], sc_ref[0], sh_ref[0]).astype(BF16)
    nt = (((1,), (1,)), ((), ()))
    q2_ref[...] = lax.dot_general(wq_ref[...], h2, nt, preferred_element_type=F32)
    tm = h2.shape[0]

    def head(h, carry):
        tops = []
        for p in range(2):
            hp = 2 * h + p
            qt = q2_ref[pl.ds(pl.multiple_of(hp * PEER_HALF, PEER_HALF), PEER_HALF), :].astype(BF16)
            s = jnp.dot(keys_ref[hp], qt, preferred_element_type=F32)
            tops.append(_topk_rows(s, PEER_TOPK))
        (s1, i1), (s2, i2) = tops
        cand_s = jnp.concatenate([s1[a:a + 1, :] + s2 for a in range(PEER_TOPK)], axis=0)
        cand_i = jnp.concatenate([i1[a:a + 1, :] * PEER_KEYS + i2 for a in range(PEER_TOPK)], axis=0)
        row = lax.broadcasted_iota(I32, cand_s.shape, 0)
        top_s, sel = [], []
        for _ in range(PEER_TOPK):
            m = jnp.max(cand_s, axis=0, keepdims=True)
            pos = jnp.min(jnp.where(cand_s == m, row, cand_s.shape[0]), axis=0, keepdims=True)
            hit = row == pos
            top_s.append(m)
            sel.append(jnp.sum(jnp.where(hit, cand_i, 0), axis=0, keepdims=True))
            cand_s = jnp.where(hit, -jnp.inf, cand_s)
        top_s = jnp.concatenate(top_s, axis=0)
        e = jnp.exp(top_s - top_s[0:1, :])
        gw = e / jnp.sum(e, axis=0, keepdims=True)
        rows_h = pl.ds(pl.multiple_of(h * PEER_TOPK, PEER_TOPK), PEER_TOPK)
        gw_ref[rows_h, :] = gw
        idxt_ref[rows_h, :] = jnp.concatenate(sel, axis=0) * ROW_SLAB
        return carry

    lax.fori_loop(0, PEER_HEADS, head, 0)
    for j in range(tm // LANES):
        idx_ref[pl.ds(j * LANES, LANES), :] = idxt_ref[:, j * LANES:(j + 1) * LANES].T


def _out_and_retrieve(conv_o, na_o, x, w_out_bf, g_post, gt1, g_pre, sc2, sh2, wq_t_bf, keys_bf):
    b, l, d = x.shape
    tm = min(256, l)
    n = b * l
    nl = l // tm
    mod_spec = pl.BlockSpec((1, 1, d), lambda i, j: (i, 0, 0))
    vec = pl.BlockSpec((1, d), lambda i, j: (0, 0))
    half = pl.BlockSpec((1, tm, D_CONV), lambda i, j: (i, j, 0))
    return pl.pallas_call(
        _outproj_kernel,
        grid=(b, nl),
        in_specs=[half, half,
                  pl.BlockSpec((1, tm, d), lambda i, j: (i, j, 0)),
                  pl.BlockSpec(w_out_bf.shape, lambda i, j: (0, 0)),
                  vec, mod_spec, vec, mod_spec, mod_spec,
                  pl.BlockSpec(wq_t_bf.shape, lambda i, j: (0, 0)),
                  pl.BlockSpec(keys_bf.shape, lambda i, j: (0, 0, 0))],
        out_specs=[pl.BlockSpec((1, tm, d), lambda i, j: (i, j, 0)),
                   pl.BlockSpec((tm, N_PAIRS), lambda i, j: (i * nl + j, 0)),
                   pl.BlockSpec((N_PAIRS, tm), lambda i, j: (0, i * nl + j))],
        out_shape=[jax.ShapeDtypeStruct((b, l, d), F32),
                   jax.ShapeDtypeStruct((n, N_PAIRS), I32),
                   jax.ShapeDtypeStruct((N_PAIRS, n), F32)],
        scratch_shapes=[pltpu.VMEM((wq_t_bf.shape[0], tm), F32),
                        pltpu.VMEM((N_PAIRS, tm), I32)],
        compiler_params=_params(2),
        name="out_proj_retrieve",
    )(conv_o, na_o, x, w_out_bf, g_post, gt1, g_pre, sc2, sh2, wq_t_bf, keys_bf)


PEER_TILE = 128


def _pack_table(tab):
    e = tab.shape[0]
    bits = lax.bitcast_convert_type(tab.astype(BF16), jnp.uint16).astype(jnp.uint32)
    words = bits[:, :ROW_WORDS] | (bits[:, ROW_WORDS:] << 16)
    return lax.bitcast_convert_type(words, I32).reshape(e * ROW_SLAB, LANES)


def _gather_rows(idx_ref, tab_ref, tile_ref, t):
    for m in range(N_PAIRS):
        row = pl.multiple_of(idx_ref[t, m], ROW_SLAB)
        tile_ref[pl.ds(ROW_SLAB * m, ROW_SLAB), :] = tab_ref[pl.ds(row, ROW_SLAB), :]


def _chunk(tile_ref, g, c):
    w = tile_ref[pl.ds(SUBLANES * ROW_SLAB * g + c, SUBLANES, stride=ROW_SLAB), :]
    lo = pltpu.bitcast(w << 16, F32)
    hi = pltpu.bitcast(w & jnp.int32(-65536), F32)
    return lo, hi


def _gelu_tanh(x):
    return x * (0.5 * (1.0 + jnp.tanh(math.sqrt(2.0 / math.pi) * (x + 0.044715 * (x * x * x)))))


def _peer_u_kernel(idx_ref, tab_ref, x1_ref, gpre_ref, sc_ref, sh_ref, gw_ref, w_ref, h_ref, tile_ref):
    h_ref[...] = _norm_mod(x1_ref[0], gpre_ref[...], sc_ref[0], sh_ref[0])
    tt = h_ref.shape[0]
    lane = lax.broadcasted_iota(I32, (N_PAIRS, tt), 1)

    def token(t, acc):
        _gather_rows(idx_ref, tab_ref, tile_ref, t)
        hrow = h_ref[pl.ds(t, 1), :]
        cols = []
        for g in range(N_PAIRS // SUBLANES):
            a = jnp.zeros((SUBLANES, LANES), F32)
            for c in range(ROW_SLAB):
                lo, hi = _chunk(tile_ref, g, c)
                a = (a + lo * hrow[:, c * LANES:(c + 1) * LANES]
                     + hi * hrow[:, ROW_WORDS + c * LANES:ROW_WORDS + (c + 1) * LANES])
            cols.append(jnp.sum(a, axis=1, keepdims=True))
        return jnp.where(lane == t, jnp.concatenate(cols, axis=0), acc)

    act = lax.fori_loop(0, tt, token, jnp.zeros((N_PAIRS, tt), F32))
    w_ref[...] = gw_ref[...] * _gelu_tanh(act)


def _peer_u(idx, u_packed, x1, g_pre, sc2, sh2, gw):
    b, l, d = x1.shape
    tt = min(PEER_TILE, l)
    nl = l // tt
    n = b * l
    return pl.pallas_call(
        _peer_u_kernel,
        grid=(b, nl),
        in_specs=[pl.BlockSpec((tt, N_PAIRS), lambda i, j: (i * nl + j, 0), memory_space=pltpu.SMEM),
                  pl.BlockSpec(memory_space=pltpu.VMEM),
                  pl.BlockSpec((1, tt, d), lambda i, j: (i, j, 0)),
                  pl.BlockSpec((1, d), lambda i, j: (0, 0)),
                  pl.BlockSpec((1, 1, d), lambda i, j: (i, 0, 0)),
                  pl.BlockSpec((1, 1, d), lambda i, j: (i, 0, 0)),
                  pl.BlockSpec((N_PAIRS, tt), lambda i, j: (0, i * nl + j))],
        out_specs=pl.BlockSpec((N_PAIRS, tt), lambda i, j: (0, i * nl + j)),
        out_shape=jax.ShapeDtypeStruct((N_PAIRS, n), F32),
        scratch_shapes=[pltpu.VMEM((tt, d), F32),
                        pltpu.VMEM((N_PAIRS * ROW_SLAB, LANES), I32)],
        compiler_params=_params(2, VMEM_TABLE_LIMIT),
        name="peer_u",
    )(idx, u_packed, x1, g_pre, sc2, sh2, gw)


def _peer_v_kernel(idx_ref, tab_ref, w_ref, x1_ref, gpost_ref, gt_ref, o_ref, f_ref, tile_ref):
    tt = f_ref.shape[0]
    lane = lax.broadcasted_iota(I32, (N_PAIRS, tt), 1)

    sub = lax.broadcasted_iota(I32, (SUBLANES, D_MODEL), 0)

    def token(t, rows):
        _gather_rows(idx_ref, tab_ref, tile_ref, t)
        wcol = jnp.sum(jnp.where(lane == t, w_ref[...], 0.0), axis=1, keepdims=True)
        wb = jnp.broadcast_to(wcol, (N_PAIRS, LANES))
        lows, highs = [], []
        for c in range(ROW_SLAB):
            acc_lo = jnp.zeros((SUBLANES, LANES), F32)
            acc_hi = jnp.zeros((SUBLANES, LANES), F32)
            for g in range(N_PAIRS // SUBLANES):
                lo, hi = _chunk(tile_ref, g, c)
                wg = wb[g * SUBLANES:(g + 1) * SUBLANES, :]
                acc_lo = acc_lo + lo * wg
                acc_hi = acc_hi + hi * wg
            lows.append(jnp.sum(acc_lo, axis=0, keepdims=True))
            highs.append(jnp.sum(acc_hi, axis=0, keepdims=True))
        frow = jnp.concatenate(lows + highs, axis=1)
        rows = jnp.where(sub == (t & (SUBLANES - 1)), jnp.broadcast_to(frow, (SUBLANES, D_MODEL)), rows)
        f_ref[pl.ds(pl.multiple_of((t >> 3) << 3, SUBLANES), SUBLANES), :] = rows
        return rows

    lax.fori_loop(0, tt, token, jnp.zeros((SUBLANES, D_MODEL), F32))
    o_ref[0] = x1_ref[0] + gt_ref[0] * (_rms(f_ref[...]) * gpost_ref[...])


def _peer_v(idx, v_packed, w, x1, g_post, gt2):
    b, l, d = x1.shape
    tt = min(PEER_TILE, l)
    nl = l // tt
    return pl.pallas_call(
        _peer_v_kernel,
        grid=(b, nl),
        in_specs=[pl.BlockSpec((tt, N_PAIRS), lambda i, j: (i * nl + j, 0), memory_space=pltpu.SMEM),
                  pl.BlockSpec(memory_space=pltpu.VMEM),
                  pl.BlockSpec((N_PAIRS, tt), lambda i, j: (0, i * nl + j)),
                  pl.BlockSpec((1, tt, d), lambda i, j: (i, j, 0)),
                  pl.BlockSpec((1, d), lambda i, j: (0, 0)),
                  pl.BlockSpec((1, 1, d), lambda i, j: (i, 0, 0))],
        out_specs=pl.BlockSpec((1, tt, d), lambda i, j: (i, j, 0)),
        out_shape=jax.ShapeDtypeStruct((b, l, d), F32),
        scratch_shapes=[pltpu.VMEM((tt, d), F32),
                        pltpu.VMEM((N_PAIRS * ROW_SLAB, LANES), I32)],
        compiler_params=_params(2, VMEM_TABLE_LIMIT),
        name="peer_v",
    )(idx, v_packed, w, x1, g_post, gt2)


def kernel(x, c, ctx, c_ctx, w_ada, b_ada, g_pre_mix, g_post_mix, g_pre_ffn, g_post_ffn, w_in, conv_w, conv_b,
           conv_norm_g, conv_norm_b, na_rpb, w_out, peer_wq, peer_keys, peer_u, peer_v):
    depth = w_ada.shape[0]
    assert depth == 1, "the context stream of deeper stacks is not implemented"
    b, l, d = x.shape
    layer = 0

    rows_c = -(-(b + 1) // SUBLANES) * SUBLANES
    c_all = jnp.zeros((rows_c, d), F32).at[:b].set(c).at[b].set(c_ctx)
    mod = _adaln(c_all, w_ada[layer], b_ada[layer])
    sh1, sc1, gt1, sh2, sc2, gt2 = [m.reshape(b, 1, d) for m in jnp.split(mod[:b], N_MOD, axis=-1)]
    csh1, csc1 = [m for m in jnp.split(mod[b:b + 1], N_MOD, axis=-1)[:2]]

    g_pre = g_pre_mix[layer].reshape(1, d)
    w_in_bf = w_in[layer].astype(BF16)
    hh, q, k, v = _in_proj(x, g_pre, sc1, sh1, w_in_bf)
    kc, vc = _ctx_proj(ctx, g_pre, csc1, csh1, w_in_bf[:, 2 * D_CONV + D_NA:])
    conv_o = _conv(hh, conv_w[layer], conv_b[layer], conv_norm_g[layer], conv_norm_b[layer])
    wr = min(NA_ROWS, l // GRID_W)
    na_o = _na_attention(q, k, v, kc, vc, _na_bias_table(na_rpb[layer], wr))

    wq_t_bf = peer_wq[layer].T.astype(BF16)
    keys_bf = peer_keys[layer].reshape(PEER_HEADS * 2, PEER_KEYS, PEER_HALF).astype(BF16)
    g_ffn = g_pre_ffn[layer].reshape(1, d)
    x1, idx, gw = _out_and_retrieve(conv_o, na_o, x, w_out[layer].astype(BF16), g_post_mix[layer].reshape(1, d),
                                    gt1, g_ffn, sc2, sh2, wq_t_bf, keys_bf)
    w = _peer_u(idx, _pack_table(peer_u[layer]), x1, g_ffn, sc2, sh2, gw)
    return _peer_v(idx, _pack_table(peer_v[layer]), w, x1, g_post_ffn[layer].reshape(1, d), gt2)
```

```python
import functools
import math

import numpy as np
import jax
import jax.numpy as jnp
from jax import lax
from jax.experimental import pallas as pl
from jax.experimental.pallas import tpu as pltpu

F32 = jnp.float32
BF16 = jnp.bfloat16
I32 = jnp.int32

D_MODEL = 1024
GRID_W = 64
D_CONV = 512
N_NA_HEADS = 8
HEAD_DIM = 64
D_NA = N_NA_HEADS * HEAD_DIM
CONV_WIDTH = 31
CONV_HALO = 16
NA_ROWS = 8
NA_COLS = 16
PEER_HEADS = 8
PEER_KEYS = 128
PEER_HALF = 128
PEER_TOPK = 16
N_PAIRS = PEER_HEADS * PEER_TOPK
N_MOD = 6
EPS = 1e-6
NEG_INF = -1e30

LANES = 128
SUBLANES = 8
ROW_WORDS = D_MODEL // 2
ROW_SLAB = ROW_WORDS // LANES
VMEM_TABLE_LIMIT = 52 * 1024 * 1024
VMEM_LIMIT = 48 * 1024 * 1024


def _silu(x):
    return x * (1.0 / (1.0 + jnp.exp(-x)))


def _rms(x):
    return x * lax.rsqrt(jnp.mean(x * x, axis=-1, keepdims=True) + EPS)


def _params(n_grid, limit=VMEM_LIMIT):
    return pltpu.CompilerParams(dimension_semantics=("arbitrary",) * n_grid, vmem_limit_bytes=limit)


def _adaln_kernel(c_ref, w_ref, b_ref, o_ref):
    s = _silu(c_ref[...]).astype(BF16)
    o_ref[...] = jnp.dot(s, w_ref[...].astype(BF16), preferred_element_type=F32) + b_ref[...]


def _adaln(c_all, w_ada, b_ada):
    rows, d = c_all.shape
    n = w_ada.shape[1]
    tn = 1536
    return pl.pallas_call(
        _adaln_kernel,
        grid=(n // tn,),
        in_specs=[pl.BlockSpec((rows, d), lambda j: (0, 0)),
                  pl.BlockSpec((d, tn), lambda j: (0, j)),
                  pl.BlockSpec((1, tn), lambda j: (0, j))],
        out_specs=pl.BlockSpec((rows, tn), lambda j: (0, j)),
        out_shape=jax.ShapeDtypeStruct((rows, n), F32),
        compiler_params=_params(1),
        name="adaln",
    )(c_all, w_ada, b_ada.reshape(1, n))


def _norm_mod(x, g, sc, sh):
    return (_rms(x) * g) * (1.0 + sc) + sh


def _inproj_kernel(x_ref, g_ref, sc_ref, sh_ref, w_ref, hh_ref, q_ref, k_ref, v_ref):
    h = _norm_mod(x_ref[0], g_ref[...], sc_ref[0], sh_ref[0]).astype(BF16)

    def proj(j):
        return jnp.dot(h, w_ref[:, j * D_CONV:(j + 1) * D_CONV], preferred_element_type=F32)

    a = proj(0)
    gate = proj(1)
    hh_ref[0] = a * (1.0 / (1.0 + jnp.exp(-gate)))
    q_ref[0] = proj(2).astype(BF16)
    k_ref[0] = proj(3).astype(BF16)
    v_ref[0] = proj(4).astype(BF16)


def _in_proj(x, g, sc, sh, w_in_bf):
    b, l, d = x.shape
    tm = min(512, l)
    mod_spec = pl.BlockSpec((1, 1, d), lambda i, j: (i, 0, 0))
    out_spec = pl.BlockSpec((1, tm, D_CONV), lambda i, j: (i, j, 0))
    return pl.pallas_call(
        _inproj_kernel,
        grid=(b, l // tm),
        in_specs=[pl.BlockSpec((1, tm, d), lambda i, j: (i, j, 0)),
                  pl.BlockSpec((1, d), lambda i, j: (0, 0)),
                  mod_spec, mod_spec,
                  pl.BlockSpec(w_in_bf.shape, lambda i, j: (0, 0))],
        out_specs=[out_spec] * 4,
        out_shape=[jax.ShapeDtypeStruct((b, l, D_CONV), F32)] + [jax.ShapeDtypeStruct((b, l, D_NA), BF16)] * 3,
        compiler_params=_params(2),
        name="in_proj",
    )(x, g, sc, sh, w_in_bf)


def _ctxproj_kernel(x_ref, g_ref, sc_ref, sh_ref, w_ref, k_ref, v_ref):
    h = _norm_mod(x_ref[0], g_ref[...], sc_ref[...], sh_ref[...]).astype(BF16)
    k_ref[0] = jnp.dot(h, w_ref[:, :D_NA], preferred_element_type=F32).astype(BF16)
    v_ref[0] = jnp.dot(h, w_ref[:, D_NA:], preferred_element_type=F32).astype(BF16)


def _ctx_proj(ctx, g, sc, sh, w_kv_bf):
    b, lc, d = ctx.shape
    vec = pl.BlockSpec((1, d), lambda i: (0, 0))
    out_spec = pl.BlockSpec((1, lc, D_NA), lambda i: (i, 0, 0))
    return pl.pallas_call(
        _ctxproj_kernel,
        grid=(b,),
        in_specs=[pl.BlockSpec((1, lc, d), lambda i: (i, 0, 0)), vec, vec, vec,
                  pl.BlockSpec(w_kv_bf.shape, lambda i: (0, 0))],
        out_specs=[out_spec] * 2,
        out_shape=[jax.ShapeDtypeStruct((b, lc, D_NA), BF16)] * 2,
        compiler_params=_params(1),
        name="ctx_proj",
    )(ctx, g, sc, sh, w_kv_bf)


CONV_TILE = 64


def _conv_kernel(h_ref, w_ref, b_ref, g_ref, beta_ref, o_ref, pad_ref):
    l = h_ref.shape[1]
    zeros = jnp.zeros((CONV_HALO, D_CONV), F32)
    pad_ref[pl.ds(0, CONV_HALO), :] = zeros
    pad_ref[pl.ds(CONV_HALO + l, CONV_HALO), :] = zeros
    pad_ref[pl.ds(CONV_HALO, l), :] = h_ref[0]
    first = CONV_HALO - CONV_WIDTH // 2

    def tile(i, carry):
        t0 = pl.multiple_of(i * CONV_TILE, CONV_TILE)
        win = pad_ref[pl.ds(t0, CONV_TILE + 2 * CONV_HALO), :]
        acc = jnp.zeros((CONV_TILE, D_CONV), F32)
        for k in range(CONV_WIDTH):
            acc = acc + win[first + k:first + k + CONV_TILE, :] * w_ref[k:k + 1, :]
        y = acc + b_ref[...]
        mu = jnp.mean(y, axis=-1, keepdims=True)
        yc = y - mu
        var = jnp.mean(yc * yc, axis=-1, keepdims=True)
        z = (yc * lax.rsqrt(var + EPS)) * g_ref[...] + beta_ref[...]
        o_ref[0, pl.ds(t0, CONV_TILE), :] = _silu(z).astype(BF16)
        return carry

    lax.fori_loop(0, l // CONV_TILE, tile, 0)


def _conv(hh, conv_w, conv_b, cn_g, cn_b):
    b, l, c = hh.shape
    vec = pl.BlockSpec((1, c), lambda i: (0, 0))
    wpad = jnp.zeros((32, c), F32).at[:CONV_WIDTH].set(conv_w)
    return pl.pallas_call(
        _conv_kernel,
        grid=(b,),
        in_specs=[pl.BlockSpec((1, l, c), lambda i: (i, 0, 0)),
                  pl.BlockSpec((32, c), lambda i: (0, 0)), vec, vec, vec],
        out_specs=pl.BlockSpec((1, l, c), lambda i: (i, 0, 0)),
        out_shape=jax.ShapeDtypeStruct((b, l, c), BF16),
        scratch_shapes=[pltpu.VMEM((l + 2 * CONV_HALO, c), F32)],
        compiler_params=_params(1),
        name="conv",
    )(hh, wpad, conv_b.reshape(1, c), cn_g.reshape(1, c), cn_b.reshape(1, c))


def _na_bias_table(rpb, wr):
    c = np.arange(GRID_W)
    win_start = np.clip(c - NA_COLS // 2, 0, GRID_W - NA_COLS)
    kc = np.arange(GRID_W)
    ok = (kc[None, :] >= win_start[:, None]) & (kc[None, :] < win_start[:, None] + NA_COLS)
    dc = kc[None, :] - c[:, None] + NA_COLS - 1
    onehot = (dc[None] == np.arange(2 * NA_COLS - 1)[:, None, None]) & ok[None]
    band = jnp.einsum("hrd,dck->hrck", rpb.astype(F32), jnp.asarray(onehot, F32),
                      precision=lax.Precision.HIGHEST)
    band = jnp.where(ok[None, None], band, NEG_INF)
    bias = jnp.stack([band[:, o:o + wr] for o in range(NA_ROWS)], axis=0)
    return bias.transpose(0, 1, 3, 2, 4).reshape(NA_ROWS, N_NA_HEADS, GRID_W, wr * GRID_W)


def _row_start(r, rows, wr):
    return jnp.clip(r - wr // 2, 0, rows - wr)


def _na_kernel(q_ref, k_ref, v_ref, kc_ref, vc_ref, bias_ref, o_ref, *, rows, wr):
    r = pl.program_id(1)
    start = pl.multiple_of(_row_start(r, rows, wr) * GRID_W, GRID_W)
    nk = wr * GRID_W
    scale = HEAD_DIM ** -0.5
    q = q_ref[0]
    lane = lax.broadcasted_iota(I32, (GRID_W, LANES), 1)
    nt = (((1,), (1,)), ((), ()))
    for p in range(N_NA_HEADS // 2):
        cols = slice(p * LANES, (p + 1) * LANES)
        qp = q[:, cols]
        kp = k_ref[0, pl.ds(start, nk), cols]
        vp = v_ref[0, pl.ds(start, nk), cols]
        kcp = kc_ref[0, :, cols]
        vcp = vc_ref[0, :, cols]
        outs = []
        for hh in range(2):
            in_head = (lane >= hh * HEAD_DIM) & (lane < (hh + 1) * HEAD_DIM)
            qm = jnp.where(in_head, qp, jnp.zeros_like(qp))
            s_loc = lax.dot_general(qm, kp, nt, preferred_element_type=F32) * scale + bias_ref[0, 2 * p + hh]
            s_ctx = lax.dot_general(qm, kcp, nt, preferred_element_type=F32) * scale
            m = jnp.maximum(jnp.max(s_loc, axis=-1, keepdims=True), jnp.max(s_ctx, axis=-1, keepdims=True))
            e_loc = jnp.exp(s_loc - m)
            e_ctx = jnp.exp(s_ctx - m)
            denom = jnp.sum(e_loc, axis=-1, keepdims=True) + jnp.sum(e_ctx, axis=-1, keepdims=True)
            o = (jnp.dot(e_loc.astype(BF16), vp, preferred_element_type=F32)
                 + jnp.dot(e_ctx.astype(BF16), vcp, preferred_element_type=F32))
            outs.append(o / denom)
        o_ref[0, :, cols] = jnp.where(lane < HEAD_DIM, outs[0], outs[1]).astype(BF16)


def _na_attention(q, k, v, kc, vc, bias):
    b, l, d = q.shape
    rows = l // GRID_W
    wr = min(NA_ROWS, rows)
    assert rows >= NA_ROWS, "neighbourhood window needs at least NA_ROWS grid rows"
    lc = kc.shape[1]

    def bias_map(i, r):
        return (_row_start(r, rows, wr) - r + NA_ROWS - 1, 0, 0, 0)

    full = pl.BlockSpec((1, l, d), lambda i, r: (i, 0, 0))
    ctx = pl.BlockSpec((1, lc, d), lambda i, r: (i, 0, 0))
    return pl.pallas_call(
        functools.partial(_na_kernel, rows=rows, wr=wr),
        grid=(b, rows),
        in_specs=[pl.BlockSpec((1, GRID_W, d), lambda i, r: (i, r, 0)), full, full, ctx, ctx,
                  pl.BlockSpec((1, N_NA_HEADS, GRID_W, wr * GRID_W), bias_map)],
        out_specs=pl.BlockSpec((1, GRID_W, d), lambda i, r: (i, r, 0)),
        out_shape=jax.ShapeDtypeStruct((b, l, d), BF16),
        compiler_params=_params(2),
        name="na_attention",
    )(q, k, v, kc, vc, bias)


_LO, _HI = slice(0, SUBLANES), slice(SUBLANES, 2 * SUBLANES)
CAND_SLABS = ([(slice(0, 1), _LO), (slice(0, 1), _HI)]
              + [(slice(a, a + 1), _LO) for a in range(1, SUBLANES)]
              + [(_HI, slice(0, 1))])
assert all((a + 1) * (b + 1) > PEER_TOPK or a == 0 or (a < SUBLANES and b < SUBLANES) or b == 0
           for a in range(PEER_TOPK) for b in range(PEER_TOPK))


def _topk_rows(x, k):
    n_rows = x.shape[0]
    row = lax.broadcasted_iota(I32, x.shape, 0)
    vals, idxs = [], []
    for _ in range(k):
        m = jnp.max(x, axis=0, keepdims=True)
        i = jnp.min(jnp.where(x == m, row, n_rows), axis=0, keepdims=True)
        vals.append(m)
        idxs.append(i)
        x = jnp.where(row == i, -jnp.inf, x)
    return jnp.concatenate(vals, axis=0), jnp.concatenate(idxs, axis=0)


def _outproj_kernel(conv_ref, na_ref, x_ref, wo_ref, gpost_ref, gt_ref, gpre_ref, sc_ref, sh_ref,
                    wq_ref, keys_ref, x1_ref, idx_ref, gw_ref, q2_ref, idxt_ref):
    y = (jnp.dot(conv_ref[0], wo_ref[:D_CONV, :], preferred_element_type=F32)
         + jnp.dot(na_ref[0], wo_ref[D_CONV:, :], preferred_element_type=F32))
    x1 = x_ref[0] + gt_ref[0] * (_rms(y) * gpost_ref[...])
    x1_ref[0] = x1
    h2 = _norm_mod(x1, gpre_ref[...], sc_ref[0], sh_ref[0]).astype(BF16)
    nt = (((1,), (1,)), ((), ()))
    q2_ref[...] = lax.dot_general(wq_ref[...], h2, nt, preferred_element_type=F32)
    tm = h2.shape[0]

    def combine(first, second, a, b):
        return first[a, :] + second[b, :]

    r = lax.broadcasted_iota(I32, (len(CAND_SLABS) * SUBLANES, tm), 0)
    slab, j = r >> 3, r & (SUBLANES - 1)
    cand_pos = jnp.where(slab < 2, r, jnp.where(slab == len(CAND_SLABS) - 1,
                                                 (SUBLANES + j) * PEER_TOPK, (slab - 1) * PEER_TOPK + j))

    def head(h, carry):
        tops = []
        for p in range(2):
            hp = 2 * h + p
            qt = q2_ref[pl.ds(pl.multiple_of(hp * PEER_HALF, PEER_HALF), PEER_HALF), :].astype(BF16)
            s = jnp.dot(keys_ref[hp], qt, preferred_element_type=F32)
            tops.append(_topk_rows(s, PEER_TOPK))
        (s1, i1), (s2, i2) = tops
        cand_s = jnp.concatenate([combine(s1, s2, a, b) for a, b in CAND_SLABS], axis=0)
        cand_i = jnp.concatenate([combine(i1 * PEER_KEYS, i2, a, b) for a, b in CAND_SLABS], axis=0)
        top_s, sel = [], []
        for _ in range(PEER_TOPK):
            m = jnp.max(cand_s, axis=0, keepdims=True)
            pos = jnp.min(jnp.where(cand_s == m, cand_pos, PEER_TOPK * PEER_TOPK), axis=0, keepdims=True)
            hit = cand_pos == pos
            top_s.append(m)
            sel.append(jnp.sum(jnp.where(hit, cand_i, 0), axis=0, keepdims=True))
            cand_s = jnp.where(hit, -jnp.inf, cand_s)
        top_s = jnp.concatenate(top_s, axis=0)
        e = jnp.exp(top_s - top_s[0:1, :])
        gw = e / jnp.sum(e, axis=0, keepdims=True)
        rows_h = pl.ds(pl.multiple_of(h * PEER_TOPK, PEER_TOPK), PEER_TOPK)
        gw_ref[rows_h, :] = gw
        idxt_ref[rows_h, :] = jnp.concatenate(sel, axis=0) * ROW_SLAB
        return carry

    lax.fori_loop(0, PEER_HEADS, head, 0)
    for j in range(tm // LANES):
        idx_ref[pl.ds(j * LANES, LANES), :] = idxt_ref[:, j * LANES:(j + 1) * LANES].T


def _out_and_retrieve(conv_o, na_o, x, w_out_bf, g_post, gt1, g_pre, sc2, sh2, wq_t_bf, keys_bf):
    b, l, d = x.shape
    tm = min(256, l)
    n = b * l
    nl = l // tm
    mod_spec = pl.BlockSpec((1, 1, d), lambda i, j: (i, 0, 0))
    vec = pl.BlockSpec((1, d), lambda i, j: (0, 0))
    half = pl.BlockSpec((1, tm, D_CONV), lambda i, j: (i, j, 0))
    return pl.pallas_call(
        _outproj_kernel,
        grid=(b, nl),
        in_specs=[half, half,
                  pl.BlockSpec((1, tm, d), lambda i, j: (i, j, 0)),
                  pl.BlockSpec(w_out_bf.shape, lambda i, j: (0, 0)),
                  vec, mod_spec, vec, mod_spec, mod_spec,
                  pl.BlockSpec(wq_t_bf.shape, lambda i, j: (0, 0)),
                  pl.BlockSpec(keys_bf.shape, lambda i, j: (0, 0, 0))],
        out_specs=[pl.BlockSpec((1, tm, d), lambda i, j: (i, j, 0)),
                   pl.BlockSpec((tm, N_PAIRS), lambda i, j: (i * nl + j, 0)),
                   pl.BlockSpec((N_PAIRS, tm), lambda i, j: (0, i * nl + j))],
        out_shape=[jax.ShapeDtypeStruct((b, l, d), F32),
                   jax.ShapeDtypeStruct((n, N_PAIRS), I32),
                   jax.ShapeDtypeStruct((N_PAIRS, n), F32)],
        scratch_shapes=[pltpu.VMEM((wq_t_bf.shape[0], tm), F32),
                        pltpu.VMEM((N_PAIRS, tm), I32)],
        compiler_params=_params(2),
        name="out_proj_retrieve",
    )(conv_o, na_o, x, w_out_bf, g_post, gt1, g_pre, sc2, sh2, wq_t_bf, keys_bf)


PEER_TILE = 128
PIPE_TOKENS = 2


def _pack_table(tab):
    e = tab.shape[0]
    bits = lax.bitcast_convert_type(tab.astype(BF16), jnp.uint16).astype(jnp.uint32)
    words = bits[:, :ROW_WORDS] | (bits[:, ROW_WORDS:] << 16)
    return lax.bitcast_convert_type(words, I32).reshape(e * ROW_SLAB, LANES)


def _gather_rows(idx_ref, tab_ref, tile_ref, t):
    for m in range(N_PAIRS):
        row = pl.multiple_of(idx_ref[t, m], ROW_SLAB)
        tile_ref[pl.ds(ROW_SLAB * m, ROW_SLAB), :] = tab_ref[pl.ds(row, ROW_SLAB), :]


def _chunk(tile_ref, g, c):
    w = tile_ref[pl.ds(SUBLANES * ROW_SLAB * g + c, SUBLANES, stride=ROW_SLAB), :]
    lo = pltpu.bitcast(w << 16, F32)
    hi = pltpu.bitcast(w & jnp.int32(-65536), F32)
    return lo, hi


def _gelu_tanh(x):
    return x * (0.5 * (1.0 + jnp.tanh(math.sqrt(2.0 / math.pi) * (x + 0.044715 * (x * x * x)))))


def _token_pipeline(idx_ref, tab_ref, tiles, tt, compute, init):
    _gather_rows(idx_ref, tab_ref, tiles[0], 0)

    def tokens(i, carry):
        t0 = PIPE_TOKENS * i
        for j in range(PIPE_TOKENS):
            _gather_rows(idx_ref, tab_ref, tiles[(j + 1) % 2], jnp.minimum(t0 + j + 1, tt - 1))
            carry = compute(t0 + j, tiles[j % 2], carry)
        return carry

    return lax.fori_loop(0, tt // PIPE_TOKENS, tokens, init)


def _peer_u_kernel(idx_ref, tab_ref, x1_ref, gpre_ref, sc_ref, sh_ref, gw_ref, w_ref, h_ref, tile_a, tile_b,
                   part_ref):
    h_ref[...] = _norm_mod(x1_ref[0], gpre_ref[...], sc_ref[0], sh_ref[0])
    tt = h_ref.shape[0]
    lane = lax.broadcasted_iota(I32, (N_PAIRS, tt), 1)

    def compute(t, tile_ref, carry):
        hrow = h_ref[pl.ds(t, 1), :]
        for g in range(N_PAIRS // SUBLANES):
            a = jnp.zeros((SUBLANES, LANES), F32)
            for c in range(ROW_SLAB):
                lo, hi = _chunk(tile_ref, g, c)
                a = (a + lo * hrow[:, c * LANES:(c + 1) * LANES]
                     + hi * hrow[:, ROW_WORDS + c * LANES:ROW_WORDS + (c + 1) * LANES])
            part_ref[t, pl.ds(g * SUBLANES, SUBLANES), :] = a
        return carry

    _token_pipeline(idx_ref, tab_ref, (tile_a, tile_b), tt, compute, 0)

    def reduce_group(i, acc):
        for j in range(SUBLANES):
            t = i * SUBLANES + j
            acc = jnp.where(lane == t, jnp.sum(part_ref[t], axis=1, keepdims=True), acc)
        return acc

    act = lax.fori_loop(0, tt // SUBLANES, reduce_group, jnp.zeros((N_PAIRS, tt), F32))
    w_ref[...] = gw_ref[...] * _gelu_tanh(act)


def _peer_u(idx, u_packed, x1, g_pre, sc2, sh2, gw):
    b, l, d = x1.shape
    tt = min(PEER_TILE, l)
    nl = l // tt
    n = b * l
    return pl.pallas_call(
        _peer_u_kernel,
        grid=(b, nl),
        in_specs=[pl.BlockSpec((tt, N_PAIRS), lambda i, j: (i * nl + j, 0), memory_space=pltpu.SMEM),
                  pl.BlockSpec(memory_space=pltpu.VMEM),
                  pl.BlockSpec((1, tt, d), lambda i, j: (i, j, 0)),
                  pl.BlockSpec((1, d), lambda i, j: (0, 0)),
                  pl.BlockSpec((1, 1, d), lambda i, j: (i, 0, 0)),
                  pl.BlockSpec((1, 1, d), lambda i, j: (i, 0, 0)),
                  pl.BlockSpec((N_PAIRS, tt), lambda i, j: (0, i * nl + j))],
        out_specs=pl.BlockSpec((N_PAIRS, tt), lambda i, j: (0, i * nl + j)),
        out_shape=jax.ShapeDtypeStruct((N_PAIRS, n), F32),
        scratch_shapes=[pltpu.VMEM((tt, d), F32),
                        pltpu.VMEM((N_PAIRS * ROW_SLAB, LANES), I32),
                        pltpu.VMEM((N_PAIRS * ROW_SLAB, LANES), I32),
                        pltpu.VMEM((tt, N_PAIRS, LANES), F32)],
        compiler_params=_params(2, VMEM_TABLE_LIMIT),
        name="peer_u",
    )(idx, u_packed, x1, g_pre, sc2, sh2, gw)


def _peer_v_kernel(idx_ref, tab_ref, w_ref, x1_ref, gpost_ref, gt_ref, o_ref, f_ref, tile_a, tile_b):
    tt = f_ref.shape[0]
    lane = lax.broadcasted_iota(I32, (N_PAIRS, tt), 1)
    sub = lax.broadcasted_iota(I32, (SUBLANES, D_MODEL), 0)

    def compute(t, tile_ref, rows):
        wcol = jnp.sum(jnp.where(lane == t, w_ref[...], 0.0), axis=1, keepdims=True)
        wb = jnp.broadcast_to(wcol, (N_PAIRS, LANES))
        lows, highs = [], []
        for c in range(ROW_SLAB):
            acc_lo = jnp.zeros((SUBLANES, LANES), F32)
            acc_hi = jnp.zeros((SUBLANES, LANES), F32)
            for g in range(N_PAIRS // SUBLANES):
                lo, hi = _chunk(tile_ref, g, c)
                wg = wb[g * SUBLANES:(g + 1) * SUBLANES, :]
                acc_lo = acc_lo + lo * wg
                acc_hi = acc_hi + hi * wg
            lows.append(jnp.sum(acc_lo, axis=0, keepdims=True))
            highs.append(jnp.sum(acc_hi, axis=0, keepdims=True))
        frow = jnp.concatenate(lows + highs, axis=1)
        rows = jnp.where(sub == (t & (SUBLANES - 1)), jnp.broadcast_to(frow, (SUBLANES, D_MODEL)), rows)
        f_ref[pl.ds(pl.multiple_of((t >> 3) << 3, SUBLANES), SUBLANES), :] = rows
        return rows

    _token_pipeline(idx_ref, tab_ref, (tile_a, tile_b), tt, compute, jnp.zeros((SUBLANES, D_MODEL), F32))
    o_ref[0] = x1_ref[0] + gt_ref[0] * (_rms(f_ref[...]) * gpost_ref[...])


def _peer_v(idx, v_packed, w, x1, g_post, gt2):
    b, l, d = x1.shape
    tt = min(PEER_TILE, l)
    nl = l // tt
    return pl.pallas_call(
        _peer_v_kernel,
        grid=(b, nl),
        in_specs=[pl.BlockSpec((tt, N_PAIRS), lambda i, j: (i * nl + j, 0), memory_space=pltpu.SMEM),
                  pl.BlockSpec(memory_space=pltpu.VMEM),
                  pl.BlockSpec((N_PAIRS, tt), lambda i, j: (0, i * nl + j)),
                  pl.BlockSpec((1, tt, d), lambda i, j: (i, j, 0)),
                  pl.BlockSpec((1, d), lambda i, j: (0, 0)),
                  pl.BlockSpec((1, 1, d), lambda i, j: (i, 0, 0))],
        out_specs=pl.BlockSpec((1, tt, d), lambda i, j: (i, j, 0)),
        out_shape=jax.ShapeDtypeStruct((b, l, d), F32),
        scratch_shapes=[pltpu.VMEM((tt, d), F32),
                        pltpu.VMEM((N_PAIRS * ROW_SLAB, LANES), I32),
                        pltpu.VMEM((N_PAIRS * ROW_SLAB, LANES), I32)],
        compiler_params=_params(2, VMEM_TABLE_LIMIT),
        name="peer_v",
    )(idx, v_packed, w, x1, g_post, gt2)


def kernel(x, c, ctx, c_ctx, w_ada, b_ada, g_pre_mix, g_post_mix, g_pre_ffn, g_post_ffn, w_in, conv_w, conv_b,
           conv_norm_g, conv_norm_b, na_rpb, w_out, peer_wq, peer_keys, peer_u, peer_v):
    depth = w_ada.shape[0]
    assert depth == 1, "the context stream of deeper stacks is not implemented"
    b, l, d = x.shape
    layer = 0

    rows_c = -(-(b + 1) // SUBLANES) * SUBLANES
    c_all = jnp.zeros((rows_c, d), F32).at[:b].set(c).at[b].set(c_ctx)
    mod = _adaln(c_all, w_ada[layer], b_ada[layer])
    sh1, sc1, gt1, sh2, sc2, gt2 = [m.reshape(b, 1, d) for m in jnp.split(mod[:b], N_MOD, axis=-1)]
    csh1, csc1 = [m for m in jnp.split(mod[b:b + 1], N_MOD, axis=-1)[:2]]

    g_pre = g_pre_mix[layer].reshape(1, d)
    w_in_bf = w_in[layer].astype(BF16)
    hh, q, k, v = _in_proj(x, g_pre, sc1, sh1, w_in_bf)
    kc, vc = _ctx_proj(ctx, g_pre, csc1, csh1, w_in_bf[:, 2 * D_CONV + D_NA:])
    conv_o = _conv(hh, conv_w[layer], conv_b[layer], conv_norm_g[layer], conv_norm_b[layer])
    wr = min(NA_ROWS, l // GRID_W)
    na_o = _na_attention(q, k, v, kc, vc, _na_bias_table(na_rpb[layer], wr))

    wq_t_bf = peer_wq[layer].T.astype(BF16)
    keys_bf = peer_keys[layer].reshape(PEER_HEADS * 2, PEER_KEYS, PEER_HALF).astype(BF16)
    g_ffn = g_pre_ffn[layer].reshape(1, d)
    x1, idx, gw = _out_and_retrieve(conv_o, na_o, x, w_out[layer].astype(BF16), g_post_mix[layer].reshape(1, d),
                                    gt1, g_ffn, sc2, sh2, wq_t_bf, keys_bf)
    w = _peer_u(idx, _pack_table(peer_u[layer]), x1, g_ffn, sc2, sh2, gw)
    return _peer_v(idx, _pack_table(peer_v[layer]), w, x1, g_post_ffn[layer].reshape(1, d), gt2)
```

```python
import functools
import math

import numpy as np
import jax
import jax.numpy as jnp
from jax import lax
from jax.experimental import pallas as pl
from jax.experimental.pallas import tpu as pltpu

F32 = jnp.float32
BF16 = jnp.bfloat16
I32 = jnp.int32

D_MODEL = 1024
GRID_W = 64
D_CONV = 512
N_NA_HEADS = 8
HEAD_DIM = 64
D_NA = N_NA_HEADS * HEAD_DIM
CONV_WIDTH = 31
CONV_HALO = 16
NA_ROWS = 8
NA_COLS = 16
PEER_HEADS = 8
PEER_KEYS = 128
PEER_HALF = 128
PEER_TOPK = 16
N_PAIRS = PEER_HEADS * PEER_TOPK
N_MOD = 6
EPS = 1e-6
NEG_INF = -1e30

LANES = 128
SUBLANES = 8
ROW_WORDS = D_MODEL // 2
ROW_SLAB = ROW_WORDS // LANES
VMEM_TABLE_LIMIT = 52 * 1024 * 1024
VMEM_LIMIT = 48 * 1024 * 1024


def _silu(x):
    return x * (1.0 / (1.0 + jnp.exp(-x)))


def _rms(x):
    return x * lax.rsqrt(jnp.mean(x * x, axis=-1, keepdims=True) + EPS)


def _params(n_grid, limit=VMEM_LIMIT):
    return pltpu.CompilerParams(dimension_semantics=("arbitrary",) * n_grid, vmem_limit_bytes=limit)


def _adaln_kernel(c_ref, w_ref, b_ref, o_ref):
    s = _silu(c_ref[...]).astype(BF16)
    o_ref[...] = jnp.dot(s, w_ref[...].astype(BF16), preferred_element_type=F32) + b_ref[...]


def _adaln(c_all, w_ada, b_ada):
    rows, d = c_all.shape
    n = w_ada.shape[1]
    tn = 1536
    return pl.pallas_call(
        _adaln_kernel,
        grid=(n // tn,),
        in_specs=[pl.BlockSpec((rows, d), lambda j: (0, 0)),
                  pl.BlockSpec((d, tn), lambda j: (0, j)),
                  pl.BlockSpec((1, tn), lambda j: (0, j))],
        out_specs=pl.BlockSpec((rows, tn), lambda j: (0, j)),
        out_shape=jax.ShapeDtypeStruct((rows, n), F32),
        compiler_params=_params(1),
        name="adaln",
    )(c_all, w_ada, b_ada.reshape(1, n))


def _norm_mod(x, g, sc, sh):
    return (_rms(x) * g) * (1.0 + sc) + sh


def _inproj_kernel(x_ref, g_ref, sc_ref, sh_ref, w_ref, hh_ref, q_ref, k_ref, v_ref):
    h = _norm_mod(x_ref[0], g_ref[...], sc_ref[0], sh_ref[0]).astype(BF16)

    def proj(j):
        return jnp.dot(h, w_ref[:, j * D_CONV:(j + 1) * D_CONV], preferred_element_type=F32)

    a = proj(0)
    gate = proj(1)
    hh_ref[0] = a * (1.0 / (1.0 + jnp.exp(-gate)))
    q_ref[0] = proj(2).astype(BF16)
    k_ref[0] = proj(3).astype(BF16)
    v_ref[0] = proj(4).astype(BF16)


def _in_proj(x, g, sc, sh, w_in_bf):
    b, l, d = x.shape
    tm = min(512, l)
    mod_spec = pl.BlockSpec((1, 1, d), lambda i, j: (i, 0, 0))
    out_spec = pl.BlockSpec((1, tm, D_CONV), lambda i, j: (i, j, 0))
    return pl.pallas_call(
        _inproj_kernel,
        grid=(b, l // tm),
        in_specs=[pl.BlockSpec((1, tm, d), lambda i, j: (i, j, 0)),
                  pl.BlockSpec((1, d), lambda i, j: (0, 0)),
                  mod_spec, mod_spec,
                  pl.BlockSpec(w_in_bf.shape, lambda i, j: (0, 0))],
        out_specs=[out_spec] * 4,
        out_shape=[jax.ShapeDtypeStruct((b, l, D_CONV), F32)] + [jax.ShapeDtypeStruct((b, l, D_NA), BF16)] * 3,
        compiler_params=_params(2),
        name="in_proj",
    )(x, g, sc, sh, w_in_bf)


def _ctxproj_kernel(x_ref, g_ref, sc_ref, sh_ref, w_ref, k_ref, v_ref):
    h = _norm_mod(x_ref[0], g_ref[...], sc_ref[...], sh_ref[...]).astype(BF16)
    k_ref[0] = jnp.dot(h, w_ref[:, :D_NA], preferred_element_type=F32).astype(BF16)
    v_ref[0] = jnp.dot(h, w_ref[:, D_NA:], preferred_element_type=F32).astype(BF16)


def _ctx_proj(ctx, g, sc, sh, w_kv_bf):
    b, lc, d = ctx.shape
    vec = pl.BlockSpec((1, d), lambda i: (0, 0))
    out_spec = pl.BlockSpec((1, lc, D_NA), lambda i: (i, 0, 0))
    return pl.pallas_call(
        _ctxproj_kernel,
        grid=(b,),
        in_specs=[pl.BlockSpec((1, lc, d), lambda i: (i, 0, 0)), vec, vec, vec,
                  pl.BlockSpec(w_kv_bf.shape, lambda i: (0, 0))],
        out_specs=[out_spec] * 2,
        out_shape=[jax.ShapeDtypeStruct((b, lc, D_NA), BF16)] * 2,
        compiler_params=_params(1),
        name="ctx_proj",
    )(ctx, g, sc, sh, w_kv_bf)


CONV_TILE = 64


def _conv_kernel(h_ref, w_ref, b_ref, g_ref, beta_ref, o_ref, pad_ref):
    l = h_ref.shape[1]
    zeros = jnp.zeros((CONV_HALO, D_CONV), F32)
    pad_ref[pl.ds(0, CONV_HALO), :] = zeros
    pad_ref[pl.ds(CONV_HALO + l, CONV_HALO), :] = zeros
    pad_ref[pl.ds(CONV_HALO, l), :] = h_ref[0]
    first = CONV_HALO - CONV_WIDTH // 2

    def tile(i, carry):
        t0 = pl.multiple_of(i * CONV_TILE, CONV_TILE)
        n_win = CONV_TILE + 2 * CONV_HALO
        win = pad_ref[pl.ds(t0, n_win), :]
        acc = jnp.zeros((CONV_TILE, D_CONV), F32)
        for s in range(SUBLANES):
            taps = [k for k in range(CONV_WIDTH) if (first + k) % SUBLANES == s]
            shifted = win if s == 0 else pltpu.roll(win, n_win - s, axis=0)
            for k in taps:
                a = (first + k) // SUBLANES * SUBLANES
                acc = acc + shifted[a:a + CONV_TILE, :] * w_ref[k:k + 1, :]
        y = acc + b_ref[...]
        mu = jnp.mean(y, axis=-1, keepdims=True)
        yc = y - mu
        var = jnp.mean(yc * yc, axis=-1, keepdims=True)
        z = (yc * lax.rsqrt(var + EPS)) * g_ref[...] + beta_ref[...]
        o_ref[0, pl.ds(t0, CONV_TILE), :] = _silu(z).astype(BF16)
        return carry

    lax.fori_loop(0, l // CONV_TILE, tile, 0)


def _conv(hh, conv_w, conv_b, cn_g, cn_b):
    b, l, c = hh.shape
    vec = pl.BlockSpec((1, c), lambda i: (0, 0))
    wpad = jnp.zeros((32, c), F32).at[:CONV_WIDTH].set(conv_w)
    return pl.pallas_call(
        _conv_kernel,
        grid=(b,),
        in_specs=[pl.BlockSpec((1, l, c), lambda i: (i, 0, 0)),
                  pl.BlockSpec((32, c), lambda i: (0, 0)), vec, vec, vec],
        out_specs=pl.BlockSpec((1, l, c), lambda i: (i, 0, 0)),
        out_shape=jax.ShapeDtypeStruct((b, l, c), BF16),
        scratch_shapes=[pltpu.VMEM((l + 2 * CONV_HALO, c), F32)],
        compiler_params=_params(1),
        name="conv",
    )(hh, wpad, conv_b.reshape(1, c), cn_g.reshape(1, c), cn_b.reshape(1, c))


def _na_bias_table(rpb, wr):
    c = np.arange(GRID_W)
    win_start = np.clip(c - NA_COLS // 2, 0, GRID_W - NA_COLS)
    kc = np.arange(GRID_W)
    ok = (kc[None, :] >= win_start[:, None]) & (kc[None, :] < win_start[:, None] + NA_COLS)
    dc = kc[None, :] - c[:, None] + NA_COLS - 1
    onehot = (dc[None] == np.arange(2 * NA_COLS - 1)[:, None, None]) & ok[None]
    band = jnp.einsum("hrd,dck->hrck", rpb.astype(F32), jnp.asarray(onehot, F32),
                      precision=lax.Precision.HIGHEST)
    band = jnp.where(ok[None, None], band, NEG_INF)
    bias = jnp.stack([band[:, o:o + wr] for o in range(NA_ROWS)], axis=0)
    return bias.transpose(0, 1, 3, 2, 4).reshape(NA_ROWS, N_NA_HEADS, GRID_W, wr * GRID_W)


def _row_start(r, rows, wr):
    return jnp.clip(r - wr // 2, 0, rows - wr)


def _na_kernel(q_ref, k_ref, v_ref, kc_ref, vc_ref, bias_ref, o_ref, *, rows, wr):
    r = pl.program_id(1)
    start = pl.multiple_of(_row_start(r, rows, wr) * GRID_W, GRID_W)
    nk = wr * GRID_W
    scale = HEAD_DIM ** -0.5
    q = q_ref[0]
    lane = lax.broadcasted_iota(I32, (GRID_W, LANES), 1)
    nt = (((1,), (1,)), ((), ()))
    heads = [(p, hh) for p in range(N_NA_HEADS // 2) for hh in range(2)]
    scores = []
    for p, hh in heads:
        cols = slice(p * LANES, (p + 1) * LANES)
        in_head = (lane >= hh * HEAD_DIM) & (lane < (hh + 1) * HEAD_DIM)
        qm = jnp.where(in_head, q[:, cols], jnp.zeros((GRID_W, LANES), q.dtype))
        s_loc = lax.dot_general(qm, k_ref[0, pl.ds(start, nk), cols], nt, preferred_element_type=F32)
        s_ctx = lax.dot_general(qm, kc_ref[0, :, cols], nt, preferred_element_type=F32)
        scores.append((s_loc * scale + bias_ref[0, 2 * p + hh], s_ctx * scale))
    probs = []
    for s_loc, s_ctx in scores:
        m = jnp.maximum(jnp.max(s_loc, axis=-1, keepdims=True), jnp.max(s_ctx, axis=-1, keepdims=True))
        e_loc = jnp.exp(s_loc - m)
        e_ctx = jnp.exp(s_ctx - m)
        denom = jnp.sum(e_loc, axis=-1, keepdims=True) + jnp.sum(e_ctx, axis=-1, keepdims=True)
        probs.append((e_loc.astype(BF16), e_ctx.astype(BF16), denom))
    outs = []
    for (p, hh), (e_loc, e_ctx, denom) in zip(heads, probs):
        cols = slice(p * LANES, (p + 1) * LANES)
        o = (jnp.dot(e_loc, v_ref[0, pl.ds(start, nk), cols], preferred_element_type=F32)
             + jnp.dot(e_ctx, vc_ref[0, :, cols], preferred_element_type=F32))
        outs.append(o / denom)
    for p in range(N_NA_HEADS // 2):
        o_ref[0, :, p * LANES:(p + 1) * LANES] = jnp.where(lane < HEAD_DIM, outs[2 * p], outs[2 * p + 1]).astype(BF16)


def _na_attention(q, k, v, kc, vc, bias):
    b, l, d = q.shape
    rows = l // GRID_W
    wr = min(NA_ROWS, rows)
    assert rows >= NA_ROWS, "neighbourhood window needs at least NA_ROWS grid rows"
    lc = kc.shape[1]

    def bias_map(i, r):
        return (_row_start(r, rows, wr) - r + NA_ROWS - 1, 0, 0, 0)

    full = pl.BlockSpec((1, l, d), lambda i, r: (i, 0, 0))
    ctx = pl.BlockSpec((1, lc, d), lambda i, r: (i, 0, 0))
    return pl.pallas_call(
        functools.partial(_na_kernel, rows=rows, wr=wr),
        grid=(b, rows),
        in_specs=[pl.BlockSpec((1, GRID_W, d), lambda i, r: (i, r, 0)), full, full, ctx, ctx,
                  pl.BlockSpec((1, N_NA_HEADS, GRID_W, wr * GRID_W), bias_map)],
        out_specs=pl.BlockSpec((1, GRID_W, d), lambda i, r: (i, r, 0)),
        out_shape=jax.ShapeDtypeStruct((b, l, d), BF16),
        compiler_params=_params(2),
        name="na_attention",
    )(q, k, v, kc, vc, bias)


_LO, _HI = slice(0, SUBLANES), slice(SUBLANES, 2 * SUBLANES)
CAND_SLABS = ([(slice(0, 1), _LO), (slice(0, 1), _HI)]
              + [(slice(a, a + 1), _LO) for a in range(1, SUBLANES)]
              + [(_HI, slice(0, 1))])
assert all((a + 1) * (b + 1) > PEER_TOPK or a == 0 or (a < SUBLANES and b < SUBLANES) or b == 0
           for a in range(PEER_TOPK) for b in range(PEER_TOPK))


def _topk_rows(xs, k):
    n_rows = xs[0].shape[0]
    row = lax.broadcasted_iota(I32, xs[0].shape, 0)
    xs = list(xs)
    vals = [[] for _ in xs]
    idxs = [[] for _ in xs]
    for _ in range(k):
        for n, x in enumerate(xs):
            m = jnp.max(x, axis=0, keepdims=True)
            i = jnp.min(jnp.where(x == m, row, n_rows), axis=0, keepdims=True)
            vals[n].append(m)
            idxs[n].append(i)
            xs[n] = jnp.where(row == i, -jnp.inf, x)
    return [(jnp.concatenate(v, axis=0), jnp.concatenate(i, axis=0)) for v, i in zip(vals, idxs)]


def _outproj_kernel(conv_ref, na_ref, x_ref, wo_ref, gpost_ref, gt_ref, gpre_ref, sc_ref, sh_ref,
                    wq_ref, keys_ref, x1_ref, idx_ref, gw_ref, q2_ref, idxt_ref):
    y = (jnp.dot(conv_ref[0], wo_ref[:D_CONV, :], preferred_element_type=F32)
         + jnp.dot(na_ref[0], wo_ref[D_CONV:, :], preferred_element_type=F32))
    x1 = x_ref[0] + gt_ref[0] * (_rms(y) * gpost_ref[...])
    x1_ref[0] = x1
    h2 = _norm_mod(x1, gpre_ref[...], sc_ref[0], sh_ref[0]).astype(BF16)
    nt = (((1,), (1,)), ((), ()))
    q2_ref[...] = lax.dot_general(wq_ref[...], h2, nt, preferred_element_type=F32)
    tm = h2.shape[0]

    def combine(first, second, a, b):
        return first[a, :] + second[b, :]

    r = lax.broadcasted_iota(I32, (len(CAND_SLABS) * SUBLANES, tm), 0)
    slab, j = r >> 3, r & (SUBLANES - 1)
    cand_pos = jnp.where(slab < 2, r, jnp.where(slab == len(CAND_SLABS) - 1,
                                                 (SUBLANES + j) * PEER_TOPK, (slab - 1) * PEER_TOPK + j))

    def head(h, carry):
        halves = []
        for p in range(2):
            hp = 2 * h + p
            qt = q2_ref[pl.ds(pl.multiple_of(hp * PEER_HALF, PEER_HALF), PEER_HALF), :].astype(BF16)
            halves.append(jnp.dot(keys_ref[hp], qt, preferred_element_type=F32))
        (s1, i1), (s2, i2) = _topk_rows(halves, PEER_TOPK)
        cand_s = jnp.concatenate([combine(s1, s2, a, b) for a, b in CAND_SLABS], axis=0)
        cand_i = jnp.concatenate([combine(i1 * PEER_KEYS, i2, a, b) for a, b in CAND_SLABS], axis=0)
        top_s, sel = [], []
        for _ in range(PEER_TOPK):
            m = jnp.max(cand_s, axis=0, keepdims=True)
            pos = jnp.min(jnp.where(cand_s == m, cand_pos, PEER_TOPK * PEER_TOPK), axis=0, keepdims=True)
            hit = cand_pos == pos
            top_s.append(m)
            sel.append(jnp.sum(jnp.where(hit, cand_i, 0), axis=0, keepdims=True))
            cand_s = jnp.where(hit, -jnp.inf, cand_s)
        top_s = jnp.concatenate(top_s, axis=0)
        e = jnp.exp(top_s - top_s[0:1, :])
        gw = e / jnp.sum(e, axis=0, keepdims=True)
        rows_h = pl.ds(pl.multiple_of(h * PEER_TOPK, PEER_TOPK), PEER_TOPK)
        gw_ref[rows_h, :] = gw
        idxt_ref[rows_h, :] = jnp.concatenate(sel, axis=0) * ROW_SLAB
        return carry

    lax.fori_loop(0, PEER_HEADS, head, 0)
    for j in range(tm // LANES):
        idx_ref[pl.ds(j * LANES, LANES), :] = idxt_ref[:, j * LANES:(j + 1) * LANES].T


def _out_and_retrieve(conv_o, na_o, x, w_out_bf, g_post, gt1, g_pre, sc2, sh2, wq_t_bf, keys_bf):
    b, l, d = x.shape
    tm = min(256, l)
    n = b * l
    nl = l // tm
    mod_spec = pl.BlockSpec((1, 1, d), lambda i, j: (i, 0, 0))
    vec = pl.BlockSpec((1, d), lambda i, j: (0, 0))
    half = pl.BlockSpec((1, tm, D_CONV), lambda i, j: (i, j, 0))
    return pl.pallas_call(
        _outproj_kernel,
        grid=(b, nl),
        in_specs=[half, half,
                  pl.BlockSpec((1, tm, d), lambda i, j: (i, j, 0)),
                  pl.BlockSpec(w_out_bf.shape, lambda i, j: (0, 0)),
                  vec, mod_spec, vec, mod_spec, mod_spec,
                  pl.BlockSpec(wq_t_bf.shape, lambda i, j: (0, 0)),
                  pl.BlockSpec(keys_bf.shape, lambda i, j: (0, 0, 0))],
        out_specs=[pl.BlockSpec((1, tm, d), lambda i, j: (i, j, 0)),
                   pl.BlockSpec((tm, N_PAIRS), lambda i, j: (i * nl + j, 0)),
                   pl.BlockSpec((N_PAIRS, tm), lambda i, j: (0, i * nl + j))],
        out_shape=[jax.ShapeDtypeStruct((b, l, d), F32),
                   jax.ShapeDtypeStruct((n, N_PAIRS), I32),
                   jax.ShapeDtypeStruct((N_PAIRS, n), F32)],
        scratch_shapes=[pltpu.VMEM((wq_t_bf.shape[0], tm), F32),
                        pltpu.VMEM((N_PAIRS, tm), I32)],
        compiler_params=_params(2),
        name="out_proj_retrieve",
    )(conv_o, na_o, x, w_out_bf, g_post, gt1, g_pre, sc2, sh2, wq_t_bf, keys_bf)


PEER_TILE = 128
PIPE_TOKENS = 2


def _pack_table(tab):
    e = tab.shape[0]
    bits = lax.bitcast_convert_type(tab.astype(BF16), jnp.uint16).astype(jnp.uint32)
    words = bits[:, :ROW_WORDS] | (bits[:, ROW_WORDS:] << 16)
    return lax.bitcast_convert_type(words, I32).reshape(e * ROW_SLAB, LANES)


def _gather_rows(idx_ref, offs, tab_ref, tile_ref, t, first=0, last=N_PAIRS):
    assert first % SUBLANES == 0 and last % SUBLANES == 0
    for m0 in range(first, last, SUBLANES):
        view = idx_ref.at[t, pl.ds(m0, SUBLANES)]
        for k in range(SUBLANES):
            row = pl.multiple_of(view[offs[k]], ROW_SLAB)
            tile_ref[pl.ds(ROW_SLAB * (m0 + k), ROW_SLAB), :] = tab_ref[pl.ds(row, ROW_SLAB), :]


def _chunk(tile_ref, g, c):
    w = tile_ref[pl.ds(SUBLANES * ROW_SLAB * g + c, SUBLANES, stride=ROW_SLAB), :]
    lo = pltpu.bitcast(w << 16, F32)
    hi = pltpu.bitcast(w & jnp.int32(-65536), F32)
    return lo, hi


def _gelu_tanh(x):
    return x * (0.5 * (1.0 + jnp.tanh(math.sqrt(2.0 / math.pi) * (x + 0.044715 * (x * x * x)))))


def _row_tiles():
    return [pltpu.VMEM((N_PAIRS * ROW_SLAB, LANES), I32) for _ in range(2 * PIPE_TOKENS)]


def _token_pipeline(idx_ref, off_ref, tab_ref, tiles, tt, compute, init, n_parts):
    offs = [off_ref[k] for k in range(SUBLANES)]
    sets = (tiles[:PIPE_TOKENS], tiles[PIPE_TOKENS:])
    for j in range(PIPE_TOKENS):
        _gather_rows(idx_ref, offs, tab_ref, sets[0][j], j)
    per_part = N_PAIRS // n_parts

    def trip(i, carry):
        for s in range(2):
            t0 = (2 * i + s) * PIPE_TOKENS
            for j in range(PIPE_TOKENS):
                t_next = jnp.minimum(t0 + PIPE_TOKENS + j, tt - 1)

                def gather_part(k, s=s, j=j, t_next=t_next):
                    _gather_rows(idx_ref, offs, tab_ref, sets[1 - s][j], t_next, k * per_part, (k + 1) * per_part)

                carry = compute(t0 + j, sets[s][j], carry, gather_part)
        return carry

    return lax.fori_loop(0, tt // (2 * PIPE_TOKENS), trip, init)


def _peer_u_kernel(idx_ref, off_ref, tab_ref, x1_ref, gpre_ref, sc_ref, sh_ref, gw_ref, w_ref, h_ref, part_ref,
                   *tiles):
    h_ref[...] = _norm_mod(x1_ref[0], gpre_ref[...], sc_ref[0], sh_ref[0])
    tt = h_ref.shape[0]
    lane = lax.broadcasted_iota(I32, (N_PAIRS, tt), 1)

    def compute(t, tile_ref, carry, gather_part):
        hrow = h_ref[pl.ds(t, 1), :]
        for g in range(N_PAIRS // SUBLANES):
            gather_part(g)
            a = jnp.zeros((SUBLANES, LANES), F32)
            for c in range(ROW_SLAB):
                lo, hi = _chunk(tile_ref, g, c)
                a = (a + lo * hrow[:, c * LANES:(c + 1) * LANES]
                     + hi * hrow[:, ROW_WORDS + c * LANES:ROW_WORDS + (c + 1) * LANES])
            part_ref[t, pl.ds(g * SUBLANES, SUBLANES), :] = a
        return carry

    _token_pipeline(idx_ref, off_ref, tab_ref, tiles, tt, compute, 0, N_PAIRS // SUBLANES)

    def reduce_group(i, acc):
        for j in range(SUBLANES):
            t = i * SUBLANES + j
            acc = jnp.where(lane == t, jnp.sum(part_ref[t], axis=1, keepdims=True), acc)
        return acc

    act = lax.fori_loop(0, tt // SUBLANES, reduce_group, jnp.zeros((N_PAIRS, tt), F32))
    w_ref[...] = gw_ref[...] * _gelu_tanh(act)


def _peer_u(idx, u_packed, x1, g_pre, sc2, sh2, gw):
    b, l, d = x1.shape
    tt = min(PEER_TILE, l)
    nl = l // tt
    n = b * l
    return pl.pallas_call(
        _peer_u_kernel,
        grid=(b, nl),
        in_specs=[pl.BlockSpec((tt, N_PAIRS), lambda i, j: (i * nl + j, 0), memory_space=pltpu.SMEM),
                  pl.BlockSpec(memory_space=pltpu.SMEM),
                  pl.BlockSpec(memory_space=pltpu.VMEM),
                  pl.BlockSpec((1, tt, d), lambda i, j: (i, j, 0)),
                  pl.BlockSpec((1, d), lambda i, j: (0, 0)),
                  pl.BlockSpec((1, 1, d), lambda i, j: (i, 0, 0)),
                  pl.BlockSpec((1, 1, d), lambda i, j: (i, 0, 0)),
                  pl.BlockSpec((N_PAIRS, tt), lambda i, j: (0, i * nl + j))],
        out_specs=pl.BlockSpec((N_PAIRS, tt), lambda i, j: (0, i * nl + j)),
        out_shape=jax.ShapeDtypeStruct((N_PAIRS, n), F32),
        scratch_shapes=[pltpu.VMEM((tt, d), F32), pltpu.VMEM((tt, N_PAIRS, LANES), F32)] + _row_tiles(),
        compiler_params=_params(2, VMEM_TABLE_LIMIT),
        name="peer_u",
    )(idx, jnp.arange(SUBLANES, dtype=I32), u_packed, x1, g_pre, sc2, sh2, gw)


def _peer_v_kernel(idx_ref, off_ref, tab_ref, w_ref, x1_ref, gpost_ref, gt_ref, o_ref, f_ref, *tiles):
    tt = f_ref.shape[0]
    lane = lax.broadcasted_iota(I32, (N_PAIRS, tt), 1)
    sub = lax.broadcasted_iota(I32, (SUBLANES, D_MODEL), 0)

    def compute(t, tile_ref, rows, gather_part):
        wcol = jnp.sum(jnp.where(lane == t, w_ref[...], 0.0), axis=1, keepdims=True)
        wb = jnp.broadcast_to(wcol, (N_PAIRS, LANES))
        acc_lo = [jnp.zeros((SUBLANES, LANES), F32) for _ in range(ROW_SLAB)]
        acc_hi = [jnp.zeros((SUBLANES, LANES), F32) for _ in range(ROW_SLAB)]
        for g in range(N_PAIRS // SUBLANES):
            gather_part(g)
            wg = wb[g * SUBLANES:(g + 1) * SUBLANES, :]
            for c in range(ROW_SLAB):
                lo, hi = _chunk(tile_ref, g, c)
                acc_lo[c] = acc_lo[c] + lo * wg
                acc_hi[c] = acc_hi[c] + hi * wg
        frow = jnp.concatenate([jnp.sum(a, axis=0, keepdims=True) for a in acc_lo + acc_hi], axis=1)
        rows = jnp.where(sub == (t & (SUBLANES - 1)), jnp.broadcast_to(frow, (SUBLANES, D_MODEL)), rows)
        f_ref[pl.ds(pl.multiple_of((t >> 3) << 3, SUBLANES), SUBLANES), :] = rows
        return rows

    _token_pipeline(idx_ref, off_ref, tab_ref, tiles, tt, compute, jnp.zeros((SUBLANES, D_MODEL), F32),
                    N_PAIRS // SUBLANES)
    o_ref[0] = x1_ref[0] + gt_ref[0] * (_rms(f_ref[...]) * gpost_ref[...])


def _peer_v(idx, v_packed, w, x1, g_post, gt2):
    b, l, d = x1.shape
    tt = min(PEER_TILE, l)
    nl = l // tt
    return pl.pallas_call(
        _peer_v_kernel,
        grid=(b, nl),
        in_specs=[pl.BlockSpec((tt, N_PAIRS), lambda i, j: (i * nl + j, 0), memory_space=pltpu.SMEM),
                  pl.BlockSpec(memory_space=pltpu.SMEM),
                  pl.BlockSpec(memory_space=pltpu.VMEM),
                  pl.BlockSpec((N_PAIRS, tt), lambda i, j: (0, i * nl + j)),
                  pl.BlockSpec((1, tt, d), lambda i, j: (i, j, 0)),
                  pl.BlockSpec((1, d), lambda i, j: (0, 0)),
                  pl.BlockSpec((1, 1, d), lambda i, j: (i, 0, 0))],
        out_specs=pl.BlockSpec((1, tt, d), lambda i, j: (i, j, 0)),
        out_shape=jax.ShapeDtypeStruct((b, l, d), F32),
        scratch_shapes=[pltpu.VMEM((tt, d), F32)] + _row_tiles(),
        compiler_params=_params(2, VMEM_TABLE_LIMIT),
        name="peer_v",
    )(idx, jnp.arange(SUBLANES, dtype=I32), v_packed, w, x1, g_post, gt2)


def kernel(x, c, ctx, c_ctx, w_ada, b_ada, g_pre_mix, g_post_mix, g_pre_ffn, g_post_ffn, w_in, conv_w, conv_b,
           conv_norm_g, conv_norm_b, na_rpb, w_out, peer_wq, peer_keys, peer_u, peer_v):
    depth = w_ada.shape[0]
    assert depth == 1, "the context stream of deeper stacks is not implemented"
    b, l, d = x.shape
    layer = 0

    rows_c = -(-(b + 1) // SUBLANES) * SUBLANES
    c_all = jnp.zeros((rows_c, d), F32).at[:b].set(c).at[b].set(c_ctx)
    mod = _adaln(c_all, w_ada[layer], b_ada[layer])
    sh1, sc1, gt1, sh2, sc2, gt2 = [m.reshape(b, 1, d) for m in jnp.split(mod[:b], N_MOD, axis=-1)]
    csh1, csc1 = [m for m in jnp.split(mod[b:b + 1], N_MOD, axis=-1)[:2]]

    g_pre = g_pre_mix[layer].reshape(1, d)
    w_in_bf = w_in[layer].astype(BF16)
    hh, q, k, v = _in_proj(x, g_pre, sc1, sh1, w_in_bf)
    kc, vc = _ctx_proj(ctx, g_pre, csc1, csh1, w_in_bf[:, 2 * D_CONV + D_NA:])
    conv_o = _conv(hh, conv_w[layer], conv_b[layer], conv_norm_g[layer], conv_norm_b[layer])
    wr = min(NA_ROWS, l // GRID_W)
    na_o = _na_attention(q, k, v, kc, vc, _na_bias_table(na_rpb[layer], wr))

    wq_t_bf = peer_wq[layer].T.astype(BF16)
    keys_bf = peer_keys[layer].reshape(PEER_HEADS * 2, PEER_KEYS, PEER_HALF).astype(BF16)
    g_ffn = g_pre_ffn[layer].reshape(1, d)
    x1, idx, gw = _out_and_retrieve(conv_o, na_o, x, w_out[layer].astype(BF16), g_post_mix[layer].reshape(1, d),
                                    gt1, g_ffn, sc2, sh2, wq_t_bf, keys_bf)
    w = _peer_u(idx, _pack_table(peer_u[layer]), x1, g_ffn, sc2, sh2, gw)
    return _peer_v(idx, _pack_table(peer_v[layer]), w, x1, g_post_ffn[layer].reshape(1, d), gt2)
```

```python
import functools
import math

import numpy as np
import jax
import jax.numpy as jnp
from jax import lax
from jax.experimental import pallas as pl
from jax.experimental.pallas import tpu as pltpu

F32 = jnp.float32
BF16 = jnp.bfloat16
I32 = jnp.int32

D_MODEL = 1024
GRID_W = 64
D_CONV = 512
N_NA_HEADS = 8
HEAD_DIM = 64
D_NA = N_NA_HEADS * HEAD_DIM
CONV_WIDTH = 31
CONV_HALO = 16
NA_ROWS = 8
NA_COLS = 16
PEER_HEADS = 8
PEER_KEYS = 128
PEER_HALF = 128
PEER_TOPK = 16
N_PAIRS = PEER_HEADS * PEER_TOPK
N_MOD = 6
EPS = 1e-6
NEG_INF = -1e30

LANES = 128
SUBLANES = 8
ROW_WORDS = D_MODEL // 2
ROW_SLAB = ROW_WORDS // LANES
VMEM_TABLE_LIMIT = 52 * 1024 * 1024
VMEM_LIMIT = 48 * 1024 * 1024


def _silu(x):
    return x * (1.0 / (1.0 + jnp.exp(-x)))


def _rms(x):
    return x * lax.rsqrt(jnp.mean(x * x, axis=-1, keepdims=True) + EPS)


def _params(n_grid, limit=VMEM_LIMIT):
    return pltpu.CompilerParams(dimension_semantics=("arbitrary",) * n_grid, vmem_limit_bytes=limit)


def _adaln_kernel(c_ref, w_ref, b_ref, o_ref):
    s = _silu(c_ref[...]).astype(BF16)
    o_ref[...] = jnp.dot(s, w_ref[...].astype(BF16), preferred_element_type=F32) + b_ref[...]


def _adaln(c_all, w_ada, b_ada):
    rows, d = c_all.shape
    n = w_ada.shape[1]
    tn = 1536
    return pl.pallas_call(
        _adaln_kernel,
        grid=(n // tn,),
        in_specs=[pl.BlockSpec((rows, d), lambda j: (0, 0)),
                  pl.BlockSpec((d, tn), lambda j: (0, j)),
                  pl.BlockSpec((1, tn), lambda j: (0, j))],
        out_specs=pl.BlockSpec((rows, tn), lambda j: (0, j)),
        out_shape=jax.ShapeDtypeStruct((rows, n), F32),
        compiler_params=_params(1),
        name="adaln",
    )(c_all, w_ada, b_ada.reshape(1, n))


def _norm_mod(x, g, sc, sh):
    return (_rms(x) * g) * (1.0 + sc) + sh


def _inproj_kernel(x_ref, g_ref, sc_ref, sh_ref, w_ref, hh_ref, q_ref, k_ref, v_ref):
    h = _norm_mod(x_ref[0], g_ref[...], sc_ref[0], sh_ref[0]).astype(BF16)

    def proj(j):
        return jnp.dot(h, w_ref[:, j * D_CONV:(j + 1) * D_CONV], preferred_element_type=F32)

    a = proj(0)
    gate = proj(1)
    hh_ref[0] = a * (1.0 / (1.0 + jnp.exp(-gate)))
    q_ref[0] = proj(2).astype(BF16)
    k_ref[0] = proj(3).astype(BF16)
    v_ref[0] = proj(4).astype(BF16)


def _in_proj(x, g, sc, sh, w_in_bf):
    b, l, d = x.shape
    tm = min(512, l)
    mod_spec = pl.BlockSpec((1, 1, d), lambda i, j: (i, 0, 0))
    out_spec = pl.BlockSpec((1, tm, D_CONV), lambda i, j: (i, j, 0))
    return pl.pallas_call(
        _inproj_kernel,
        grid=(b, l // tm),
        in_specs=[pl.BlockSpec((1, tm, d), lambda i, j: (i, j, 0)),
                  pl.BlockSpec((1, d), lambda i, j: (0, 0)),
                  mod_spec, mod_spec,
                  pl.BlockSpec(w_in_bf.shape, lambda i, j: (0, 0))],
        out_specs=[out_spec] * 4,
        out_shape=[jax.ShapeDtypeStruct((b, l, D_CONV), F32)] + [jax.ShapeDtypeStruct((b, l, D_NA), BF16)] * 3,
        compiler_params=_params(2),
        name="in_proj",
    )(x, g, sc, sh, w_in_bf)


def _ctxproj_kernel(x_ref, g_ref, sc_ref, sh_ref, w_ref, k_ref, v_ref):
    h = _norm_mod(x_ref[0], g_ref[...], sc_ref[...], sh_ref[...]).astype(BF16)
    k_ref[0] = jnp.dot(h, w_ref[:, :D_NA], preferred_element_type=F32).astype(BF16)
    v_ref[0] = jnp.dot(h, w_ref[:, D_NA:], preferred_element_type=F32).astype(BF16)


def _ctx_proj(ctx, g, sc, sh, w_kv_bf):
    b, lc, d = ctx.shape
    vec = pl.BlockSpec((1, d), lambda i: (0, 0))
    out_spec = pl.BlockSpec((1, lc, D_NA), lambda i: (i, 0, 0))
    return pl.pallas_call(
        _ctxproj_kernel,
        grid=(b,),
        in_specs=[pl.BlockSpec((1, lc, d), lambda i: (i, 0, 0)), vec, vec, vec,
                  pl.BlockSpec(w_kv_bf.shape, lambda i: (0, 0))],
        out_specs=[out_spec] * 2,
        out_shape=[jax.ShapeDtypeStruct((b, lc, D_NA), BF16)] * 2,
        compiler_params=_params(1),
        name="ctx_proj",
    )(ctx, g, sc, sh, w_kv_bf)


CONV_TILE = 64


def _conv_kernel(h_ref, w_ref, b_ref, g_ref, beta_ref, o_ref, pad_ref):
    l = h_ref.shape[1]
    zeros = jnp.zeros((CONV_HALO, D_CONV), F32)
    pad_ref[pl.ds(0, CONV_HALO), :] = zeros
    pad_ref[pl.ds(CONV_HALO + l, CONV_HALO), :] = zeros
    pad_ref[pl.ds(CONV_HALO, l), :] = h_ref[0]
    first = CONV_HALO - CONV_WIDTH // 2

    def tile(i, carry):
        t0 = pl.multiple_of(i * CONV_TILE, CONV_TILE)
        n_win = CONV_TILE + 2 * CONV_HALO
        win = pad_ref[pl.ds(t0, n_win), :]
        acc = jnp.zeros((CONV_TILE, D_CONV), F32)
        for s in range(SUBLANES):
            taps = [k for k in range(CONV_WIDTH) if (first + k) % SUBLANES == s]
            shifted = win if s == 0 else pltpu.roll(win, n_win - s, axis=0)
            for k in taps:
                a = (first + k) // SUBLANES * SUBLANES
                acc = acc + shifted[a:a + CONV_TILE, :] * w_ref[k:k + 1, :]
        y = acc + b_ref[...]
        mu = jnp.mean(y, axis=-1, keepdims=True)
        yc = y - mu
        var = jnp.mean(yc * yc, axis=-1, keepdims=True)
        z = (yc * lax.rsqrt(var + EPS)) * g_ref[...] + beta_ref[...]
        o_ref[0, pl.ds(t0, CONV_TILE), :] = _silu(z).astype(BF16)
        return carry

    lax.fori_loop(0, l // CONV_TILE, tile, 0)


def _conv(hh, conv_w, conv_b, cn_g, cn_b):
    b, l, c = hh.shape
    vec = pl.BlockSpec((1, c), lambda i: (0, 0))
    wpad = jnp.zeros((32, c), F32).at[:CONV_WIDTH].set(conv_w)
    return pl.pallas_call(
        _conv_kernel,
        grid=(b,),
        in_specs=[pl.BlockSpec((1, l, c), lambda i: (i, 0, 0)),
                  pl.BlockSpec((32, c), lambda i: (0, 0)), vec, vec, vec],
        out_specs=pl.BlockSpec((1, l, c), lambda i: (i, 0, 0)),
        out_shape=jax.ShapeDtypeStruct((b, l, c), BF16),
        scratch_shapes=[pltpu.VMEM((l + 2 * CONV_HALO, c), F32)],
        compiler_params=_params(1),
        name="conv",
    )(hh, wpad, conv_b.reshape(1, c), cn_g.reshape(1, c), cn_b.reshape(1, c))


def _na_bias_table(rpb, wr):
    c = np.arange(GRID_W)
    win_start = np.clip(c - NA_COLS // 2, 0, GRID_W - NA_COLS)
    kc = np.arange(GRID_W)
    ok = (kc[None, :] >= win_start[:, None]) & (kc[None, :] < win_start[:, None] + NA_COLS)
    dc = kc[None, :] - c[:, None] + NA_COLS - 1
    onehot = (dc[None] == np.arange(2 * NA_COLS - 1)[:, None, None]) & ok[None]
    band = jnp.einsum("hrd,dck->hrck", rpb.astype(F32), jnp.asarray(onehot, F32),
                      precision=lax.Precision.HIGHEST)
    band = jnp.where(ok[None, None], band, NEG_INF)
    bias = jnp.stack([band[:, o:o + wr] for o in range(NA_ROWS)], axis=0)
    return bias.transpose(0, 1, 3, 2, 4).reshape(NA_ROWS, N_NA_HEADS, GRID_W, wr * GRID_W)


def _row_start(r, rows, wr):
    return jnp.clip(r - wr // 2, 0, rows - wr)


NA_STEP_ROWS = 8


def _na_kernel(q_ref, k_ref, v_ref, kc_ref, vc_ref, bias_ref, o_ref, *, rows, wr):
    nk = wr * GRID_W
    scale = HEAD_DIM ** -0.5
    lane = lax.broadcasted_iota(I32, (GRID_W, LANES), 1)
    nt = (((1,), (1,)), ((), ()))
    heads = [(p, hh) for p in range(N_NA_HEADS // 2) for hh in range(2)]

    def one_row(rr, carry):
        r = pl.program_id(1) * NA_STEP_ROWS + rr
        row_start = _row_start(r, rows, wr)
        start = pl.multiple_of(row_start * GRID_W, GRID_W)
        offset = row_start - r + NA_ROWS - 1
        qrows = pl.ds(pl.multiple_of(rr * GRID_W, GRID_W), GRID_W)
        q = q_ref[0, qrows, :]
        scores = []
        for p, hh in heads:
            cols = slice(p * LANES, (p + 1) * LANES)
            in_head = (lane >= hh * HEAD_DIM) & (lane < (hh + 1) * HEAD_DIM)
            qm = jnp.where(in_head, q[:, cols], jnp.zeros((GRID_W, LANES), q.dtype))
            s_loc = lax.dot_general(qm, k_ref[0, pl.ds(start, nk), cols], nt, preferred_element_type=F32)
            s_ctx = lax.dot_general(qm, kc_ref[0, :, cols], nt, preferred_element_type=F32)
            scores.append((s_loc * scale + bias_ref[offset, 2 * p + hh], s_ctx * scale))
        probs = []
        for s_loc, s_ctx in scores:
            m = jnp.maximum(jnp.max(s_loc, axis=-1, keepdims=True), jnp.max(s_ctx, axis=-1, keepdims=True))
            e_loc = jnp.exp(s_loc - m)
            e_ctx = jnp.exp(s_ctx - m)
            denom = jnp.sum(e_loc, axis=-1, keepdims=True) + jnp.sum(e_ctx, axis=-1, keepdims=True)
            probs.append((e_loc.astype(BF16), e_ctx.astype(BF16), denom))
        outs = []
        for (p, hh), (e_loc, e_ctx, denom) in zip(heads, probs):
            cols = slice(p * LANES, (p + 1) * LANES)
            o = (jnp.dot(e_loc, v_ref[0, pl.ds(start, nk), cols], preferred_element_type=F32)
                 + jnp.dot(e_ctx, vc_ref[0, :, cols], preferred_element_type=F32))
            outs.append(o / denom)
        for p in range(N_NA_HEADS // 2):
            o_ref[0, qrows, p * LANES:(p + 1) * LANES] = jnp.where(
                lane < HEAD_DIM, outs[2 * p], outs[2 * p + 1]).astype(BF16)
        return carry

    lax.fori_loop(0, NA_STEP_ROWS, one_row, 0)


def _na_attention(q, k, v, kc, vc, bias):
    b, l, d = q.shape
    rows = l // GRID_W
    wr = min(NA_ROWS, rows)
    assert rows >= NA_ROWS and rows % NA_STEP_ROWS == 0, "needs a multiple of NA_STEP_ROWS grid rows"
    lc = kc.shape[1]
    tq = NA_STEP_ROWS * GRID_W
    full = pl.BlockSpec((1, l, d), lambda i, r: (i, 0, 0))
    ctx = pl.BlockSpec((1, lc, d), lambda i, r: (i, 0, 0))
    return pl.pallas_call(
        functools.partial(_na_kernel, rows=rows, wr=wr),
        grid=(b, rows // NA_STEP_ROWS),
        in_specs=[pl.BlockSpec((1, tq, d), lambda i, r: (i, r, 0)), full, full, ctx, ctx,
                  pl.BlockSpec(bias.shape, lambda i, r: (0, 0, 0, 0))],
        out_specs=pl.BlockSpec((1, tq, d), lambda i, r: (i, r, 0)),
        out_shape=jax.ShapeDtypeStruct((b, l, d), BF16),
        compiler_params=_params(2),
        name="na_attention",
    )(q, k, v, kc, vc, bias)


_LO, _HI = slice(0, SUBLANES), slice(SUBLANES, 2 * SUBLANES)
CAND_SLABS = ([(slice(0, 1), _LO), (slice(0, 1), _HI)]
              + [(slice(a, a + 1), _LO) for a in range(1, SUBLANES)]
              + [(_HI, slice(0, 1))])
assert all((a + 1) * (b + 1) > PEER_TOPK or a == 0 or (a < SUBLANES and b < SUBLANES) or b == 0
           for a in range(PEER_TOPK) for b in range(PEER_TOPK))


def _topk_rows(xs, k):
    n_rows = xs[0].shape[0]
    row = lax.broadcasted_iota(I32, xs[0].shape, 0)
    xs = list(xs)
    vals = [[] for _ in xs]
    idxs = [[] for _ in xs]
    for _ in range(k):
        for n, x in enumerate(xs):
            m = jnp.max(x, axis=0, keepdims=True)
            i = jnp.min(jnp.where(x == m, row, n_rows), axis=0, keepdims=True)
            vals[n].append(m)
            idxs[n].append(i)
            xs[n] = jnp.where(row == i, -jnp.inf, x)
    return [(jnp.concatenate(v, axis=0), jnp.concatenate(i, axis=0)) for v, i in zip(vals, idxs)]


def _outproj_kernel(conv_ref, na_ref, x_ref, wo_ref, gpost_ref, gt_ref, gpre_ref, sc_ref, sh_ref,
                    wq_ref, keys_ref, x1_ref, idx_ref, gw_ref, q2_ref, idxt_ref):
    y = (jnp.dot(conv_ref[0], wo_ref[:D_CONV, :], preferred_element_type=F32)
         + jnp.dot(na_ref[0], wo_ref[D_CONV:, :], preferred_element_type=F32))
    x1 = x_ref[0] + gt_ref[0] * (_rms(y) * gpost_ref[...])
    x1_ref[0] = x1
    h2 = _norm_mod(x1, gpre_ref[...], sc_ref[0], sh_ref[0]).astype(BF16)
    nt = (((1,), (1,)), ((), ()))
    q2_ref[...] = lax.dot_general(wq_ref[...], h2, nt, preferred_element_type=F32)
    tm = h2.shape[0]

    def combine(first, second, a, b):
        return first[a, :] + second[b, :]

    r = lax.broadcasted_iota(I32, (len(CAND_SLABS) * SUBLANES, tm), 0)
    slab, j = r >> 3, r & (SUBLANES - 1)
    cand_pos = jnp.where(slab < 2, r, jnp.where(slab == len(CAND_SLABS) - 1,
                                                 (SUBLANES + j) * PEER_TOPK, (slab - 1) * PEER_TOPK + j))

    def head(h, carry):
        halves = []
        for p in range(2):
            hp = 2 * h + p
            qt = q2_ref[pl.ds(pl.multiple_of(hp * PEER_HALF, PEER_HALF), PEER_HALF), :].astype(BF16)
            halves.append(jnp.dot(keys_ref[hp], qt, preferred_element_type=F32))
        (s1, i1), (s2, i2) = _topk_rows(halves, PEER_TOPK)
        cand_s = jnp.concatenate([combine(s1, s2, a, b) for a, b in CAND_SLABS], axis=0)
        cand_i = jnp.concatenate([combine(i1 * PEER_KEYS, i2, a, b) for a, b in CAND_SLABS], axis=0)
        top_s, sel = [], []
        for _ in range(PEER_TOPK):
            m = jnp.max(cand_s, axis=0, keepdims=True)
            pos = jnp.min(jnp.where(cand_s == m, cand_pos, PEER_TOPK * PEER_TOPK), axis=0, keepdims=True)
            hit = cand_pos == pos
            top_s.append(m)
            sel.append(jnp.sum(jnp.where(hit, cand_i, 0), axis=0, keepdims=True))
            cand_s = jnp.where(hit, -jnp.inf, cand_s)
        top_s = jnp.concatenate(top_s, axis=0)
        e = jnp.exp(top_s - top_s[0:1, :])
        gw = e / jnp.sum(e, axis=0, keepdims=True)
        rows_h = pl.ds(pl.multiple_of(h * PEER_TOPK, PEER_TOPK), PEER_TOPK)
        gw_ref[rows_h, :] = gw
        idxt_ref[rows_h, :] = jnp.concatenate(sel, axis=0) * ROW_SLAB
        return carry

    lax.fori_loop(0, PEER_HEADS, head, 0)
    for j in range(tm // LANES):
        idx_ref[pl.ds(j * LANES, LANES), :] = idxt_ref[:, j * LANES:(j + 1) * LANES].T


def _out_and_retrieve(conv_o, na_o, x, w_out_bf, g_post, gt1, g_pre, sc2, sh2, wq_t_bf, keys_bf):
    b, l, d = x.shape
    tm = min(256, l)
    n = b * l
    nl = l // tm
    mod_spec = pl.BlockSpec((1, 1, d), lambda i, j: (i, 0, 0))
    vec = pl.BlockSpec((1, d), lambda i, j: (0, 0))
    half = pl.BlockSpec((1, tm, D_CONV), lambda i, j: (i, j, 0))
    return pl.pallas_call(
        _outproj_kernel,
        grid=(b, nl),
        in_specs=[half, half,
                  pl.BlockSpec((1, tm, d), lambda i, j: (i, j, 0)),
                  pl.BlockSpec(w_out_bf.shape, lambda i, j: (0, 0)),
                  vec, mod_spec, vec, mod_spec, mod_spec,
                  pl.BlockSpec(wq_t_bf.shape, lambda i, j: (0, 0)),
                  pl.BlockSpec(keys_bf.shape, lambda i, j: (0, 0, 0))],
        out_specs=[pl.BlockSpec((1, tm, d), lambda i, j: (i, j, 0)),
                   pl.BlockSpec((tm, N_PAIRS), lambda i, j: (i * nl + j, 0)),
                   pl.BlockSpec((N_PAIRS, tm), lambda i, j: (0, i * nl + j))],
        out_shape=[jax.ShapeDtypeStruct((b, l, d), F32),
                   jax.ShapeDtypeStruct((n, N_PAIRS), I32),
                   jax.ShapeDtypeStruct((N_PAIRS, n), F32)],
        scratch_shapes=[pltpu.VMEM((wq_t_bf.shape[0], tm), F32),
                        pltpu.VMEM((N_PAIRS, tm), I32)],
        compiler_params=_params(2),
        name="out_proj_retrieve",
    )(conv_o, na_o, x, w_out_bf, g_post, gt1, g_pre, sc2, sh2, wq_t_bf, keys_bf)


PEER_TILE = 128
PIPE_TOKENS = 2


def _pack_table(tab):
    e = tab.shape[0]
    bits = lax.bitcast_convert_type(tab.astype(BF16), jnp.uint16).astype(jnp.uint32)
    words = bits[:, :ROW_WORDS] | (bits[:, ROW_WORDS:] << 16)
    return lax.bitcast_convert_type(words, I32).reshape(e * ROW_SLAB, LANES)


def _gather_rows(idx_ref, offs, tab_ref, tile_ref, t, first=0, last=N_PAIRS):
    assert first % SUBLANES == 0 and last % SUBLANES == 0
    for m0 in range(first, last, SUBLANES):
        view = idx_ref.at[t, pl.ds(m0, SUBLANES)]
        for k in range(SUBLANES):
            row = pl.multiple_of(view[offs[k]], ROW_SLAB)
            tile_ref[pl.ds(ROW_SLAB * (m0 + k), ROW_SLAB), :] = tab_ref[pl.ds(row, ROW_SLAB), :]


def _chunk(tile_ref, g, c):
    w = tile_ref[pl.ds(SUBLANES * ROW_SLAB * g + c, SUBLANES, stride=ROW_SLAB), :]
    lo = pltpu.bitcast(w << 16, F32)
    hi = pltpu.bitcast(w & jnp.int32(-65536), F32)
    return lo, hi


def _gelu_tanh(x):
    return x * (0.5 * (1.0 + jnp.tanh(math.sqrt(2.0 / math.pi) * (x + 0.044715 * (x * x * x)))))


def _row_tiles():
    return [pltpu.VMEM((N_PAIRS * ROW_SLAB, LANES), I32) for _ in range(2 * PIPE_TOKENS)]


def _token_pipeline(idx_ref, off_ref, tab_ref, tiles, first, count, compute, init, n_parts):
    offs = [off_ref[k] for k in range(SUBLANES)]
    sets = (tiles[:PIPE_TOKENS], tiles[PIPE_TOKENS:])
    for j in range(PIPE_TOKENS):
        _gather_rows(idx_ref, offs, tab_ref, sets[0][j], first + j)
    per_part = N_PAIRS // n_parts

    def trip(i, carry):
        for s in range(2):
            t0 = first + (2 * i + s) * PIPE_TOKENS
            for j in range(PIPE_TOKENS):
                t_next = jnp.minimum(t0 + PIPE_TOKENS + j, first + count - 1)

                def gather_part(k, s=s, j=j, t_next=t_next):
                    _gather_rows(idx_ref, offs, tab_ref, sets[1 - s][j], t_next, k * per_part, (k + 1) * per_part)

                carry = compute(t0 + j, sets[s][j], carry, gather_part)
        return carry

    return lax.fori_loop(0, count // (2 * PIPE_TOKENS), trip, init)


def _peer_u_kernel(idx_ref, off_ref, tab_ref, x1_ref, gpre_ref, sc_ref, sh_ref, gw_ref, w_ref, h_ref, part_a, part_b,
                   red_ref, *tiles):
    tt = h_ref.shape[0]
    half = tt // 2

    @pl.when(pl.program_id(0) == 0)
    def _():
        part_b[...] = jnp.zeros(part_b.shape, F32)

    h_ref[...] = _norm_mod(x1_ref[0], gpre_ref[...], sc_ref[0], sh_ref[0])
    lane = lax.broadcasted_iota(I32, (N_PAIRS, tt), 1)

    def run_half(first, cur_ref, prev_ref, red_first):
        def compute(t, tile_ref, carry, gather_part):
            hrow = h_ref[pl.ds(t, 1), :]
            k = t - first
            for g in range(N_PAIRS // SUBLANES):
                gather_part(g)
                rows = pl.ds(g * SUBLANES, SUBLANES)
                a = jnp.zeros((SUBLANES, LANES), F32)
                for c in range(ROW_SLAB):
                    lo, hi = _chunk(tile_ref, g, c)
                    a = (a + lo * hrow[:, c * LANES:(c + 1) * LANES]
                         + hi * hrow[:, ROW_WORDS + c * LANES:ROW_WORDS + (c + 1) * LANES])
                cur_ref[k, rows, :] = a
                red_ref[red_first + k, rows, :] = jnp.broadcast_to(
                    jnp.sum(prev_ref[k, rows, :], axis=1, keepdims=True), (SUBLANES, LANES))
            return carry

        _token_pipeline(idx_ref, off_ref, tab_ref, tiles, first, half, compute, 0, N_PAIRS // SUBLANES)

    run_half(0, part_a, part_b, 0)
    run_half(half, part_b, part_a, half)

    def place_group(i, acc):
        for j in range(SUBLANES):
            t = i * SUBLANES + j
            acc = jnp.where(lane == t, red_ref[t], acc)
        return acc

    act = lax.fori_loop(0, tt // SUBLANES, place_group, jnp.zeros((N_PAIRS, tt), F32))
    w_ref[...] = gw_ref[...] * _gelu_tanh(act)


def _peer_u(idx, u_packed, x1, g_pre, sc2, sh2, gw):
    b, l, d = x1.shape
    tt = min(PEER_TILE, l)
    nl = l // tt
    n_tiles = b * nl
    half = tt // 2

    def cur(s):
        return jnp.minimum(s, n_tiles - 1)

    mod_spec = pl.BlockSpec((1, 1, d), lambda s: (cur(s) // nl, 0, 0))
    shifted = pl.BlockSpec((N_PAIRS, tt), lambda s: (0, s))
    w_shifted = pl.pallas_call(
        _peer_u_kernel,
        grid=(n_tiles + 1,),
        in_specs=[pl.BlockSpec((tt, N_PAIRS), lambda s: (cur(s), 0), memory_space=pltpu.SMEM),
                  pl.BlockSpec(memory_space=pltpu.SMEM),
                  pl.BlockSpec(memory_space=pltpu.VMEM),
                  pl.BlockSpec((1, tt, d), lambda s: (cur(s) // nl, cur(s) % nl, 0)),
                  pl.BlockSpec((1, d), lambda s: (0, 0)),
                  mod_spec, mod_spec, shifted],
        out_specs=shifted,
        out_shape=jax.ShapeDtypeStruct((N_PAIRS, (n_tiles + 1) * tt), F32),
        scratch_shapes=[pltpu.VMEM((tt, d), F32), pltpu.VMEM((half, N_PAIRS, LANES), F32),
                        pltpu.VMEM((half, N_PAIRS, LANES), F32), pltpu.VMEM((tt, N_PAIRS, LANES), F32)] + _row_tiles(),
        compiler_params=_params(1, VMEM_TABLE_LIMIT),
        name="peer_u",
    )(idx, jnp.arange(SUBLANES, dtype=I32), u_packed, x1, g_pre, sc2, sh2,
      jnp.pad(gw, ((0, 0), (half, half))))
    return w_shifted[:, half:half + n_tiles * tt]


def _peer_v_kernel(idx_ref, off_ref, tab_ref, w_ref, x1_ref, gpost_ref, gt_ref, o_ref, f_ref, *tiles):
    tt = f_ref.shape[0]
    lane = lax.broadcasted_iota(I32, (N_PAIRS, tt), 1)
    sub = lax.broadcasted_iota(I32, (SUBLANES, D_MODEL), 0)

    def compute(t, tile_ref, rows, gather_part):
        wcol = jnp.sum(jnp.where(lane == t, w_ref[...], 0.0), axis=1, keepdims=True)
        wb = jnp.broadcast_to(wcol, (N_PAIRS, LANES))
        acc_lo = [jnp.zeros((SUBLANES, LANES), F32) for _ in range(ROW_SLAB)]
        acc_hi = [jnp.zeros((SUBLANES, LANES), F32) for _ in range(ROW_SLAB)]
        for g in range(N_PAIRS // SUBLANES):
            gather_part(g)
            wg = wb[g * SUBLANES:(g + 1) * SUBLANES, :]
            for c in range(ROW_SLAB):
                lo, hi = _chunk(tile_ref, g, c)
                acc_lo[c] = acc_lo[c] + lo * wg
                acc_hi[c] = acc_hi[c] + hi * wg
        frow = jnp.concatenate([jnp.sum(a, axis=0, keepdims=True) for a in acc_lo + acc_hi], axis=1)
        rows = jnp.where(sub == (t & (SUBLANES - 1)), jnp.broadcast_to(frow, (SUBLANES, D_MODEL)), rows)
        f_ref[pl.ds(pl.multiple_of((t >> 3) << 3, SUBLANES), SUBLANES), :] = rows
        return rows

    _token_pipeline(idx_ref, off_ref, tab_ref, tiles, 0, tt, compute, jnp.zeros((SUBLANES, D_MODEL), F32),
                    N_PAIRS // SUBLANES)
    o_ref[0] = x1_ref[0] + gt_ref[0] * (_rms(f_ref[...]) * gpost_ref[...])


def _peer_v(idx, v_packed, w, x1, g_post, gt2):
    b, l, d = x1.shape
    tt = min(PEER_TILE, l)
    nl = l // tt
    return pl.pallas_call(
        _peer_v_kernel,
        grid=(b, nl),
        in_specs=[pl.BlockSpec((tt, N_PAIRS), lambda i, j: (i * nl + j, 0), memory_space=pltpu.SMEM),
                  pl.BlockSpec(memory_space=pltpu.SMEM),
                  pl.BlockSpec(memory_space=pltpu.VMEM),
                  pl.BlockSpec((N_PAIRS, tt), lambda i, j: (0, i * nl + j)),
                  pl.BlockSpec((1, tt, d), lambda i, j: (i, j, 0)),
                  pl.BlockSpec((1, d), lambda i, j: (0, 0)),
                  pl.BlockSpec((1, 1, d), lambda i, j: (i, 0, 0))],
        out_specs=pl.BlockSpec((1, tt, d), lambda i, j: (i, j, 0)),
        out_shape=jax.ShapeDtypeStruct((b, l, d), F32),
        scratch_shapes=[pltpu.VMEM((tt, d), F32)] + _row_tiles(),
        compiler_params=_params(2, VMEM_TABLE_LIMIT),
        name="peer_v",
    )(idx, jnp.arange(SUBLANES, dtype=I32), v_packed, w, x1, g_post, gt2)


def kernel(x, c, ctx, c_ctx, w_ada, b_ada, g_pre_mix, g_post_mix, g_pre_ffn, g_post_ffn, w_in, conv_w, conv_b,
           conv_norm_g, conv_norm_b, na_rpb, w_out, peer_wq, peer_keys, peer_u, peer_v):
    depth = w_ada.shape[0]
    assert depth == 1, "the context stream of deeper stacks is not implemented"
    b, l, d = x.shape
    layer = 0

    rows_c = -(-(b + 1) // SUBLANES) * SUBLANES
    c_all = jnp.zeros((rows_c, d), F32).at[:b].set(c).at[b].set(c_ctx)
    mod = _adaln(c_all, w_ada[layer], b_ada[layer])
    sh1, sc1, gt1, sh2, sc2, gt2 = [m.reshape(b, 1, d) for m in jnp.split(mod[:b], N_MOD, axis=-1)]
    csh1, csc1 = [m for m in jnp.split(mod[b:b + 1], N_MOD, axis=-1)[:2]]

    g_pre = g_pre_mix[layer].reshape(1, d)
    w_in_bf = w_in[layer].astype(BF16)
    hh, q, k, v = _in_proj(x, g_pre, sc1, sh1, w_in_bf)
    kc, vc = _ctx_proj(ctx, g_pre, csc1, csh1, w_in_bf[:, 2 * D_CONV + D_NA:])
    conv_o = _conv(hh, conv_w[layer], conv_b[layer], conv_norm_g[layer], conv_norm_b[layer])
    wr = min(NA_ROWS, l // GRID_W)
    na_o = _na_attention(q, k, v, kc, vc, _na_bias_table(na_rpb[layer], wr))

    wq_t_bf = peer_wq[layer].T.astype(BF16)
    keys_bf = peer_keys[layer].reshape(PEER_HEADS * 2, PEER_KEYS, PEER_HALF).astype(BF16)
    g_ffn = g_pre_ffn[layer].reshape(1, d)
    x1, idx, gw = _out_and_retrieve(conv_o, na_o, x, w_out[layer].astype(BF16), g_post_mix[layer].reshape(1, d),
                                    gt1, g_ffn, sc2, sh2, wq_t_bf, keys_bf)
    w = _peer_u(idx, _pack_table(peer_u[layer]), x1, g_ffn, sc2, sh2, gw)
    return _peer_v(idx, _pack_table(peer_v[layer]), w, x1, g_post_ffn[layer].reshape(1, d), gt2)
```
